```python
import math
import functools
import jax
import jax.numpy as jnp
from jax import lax
import numpy as np

D_MODEL = 4096
BATCH = 4
SEQ = 2048
DEPTH = 2
DEC_BATCH = 8
DEC_SEQ = 4
PAST_LEN = 16384
PAGE_SIZE = 128

DA_HEADS = D_MODEL // 512
DA_KV_HEADS = DA_HEADS // 2
DA_HEAD_DIM = 128
DA_Q = DA_HEADS * 2 * DA_HEAD_DIM
DA_K = DA_KV_HEADS * 2 * DA_HEAD_DIM
DA_V = DA_KV_HEADS * 2 * DA_HEAD_DIM
DA_OUT = DA_HEADS * 2 * DA_HEAD_DIM
MB_HEADS = D_MODEL // 512
MB_HEAD_DIM = 128
MB_W = MB_HEADS * MB_HEAD_DIM
MB_BLOCK = 256
MB_TOPK = 3
MB_Q_CHUNK = 16
CONV_DIM = D_MODEL // 4
CONV_W = 3
IN_WIDTH = DA_Q + DA_K + DA_V + 3 * MB_W + 3 * CONV_DIM
PEER_HEADS = 8
PEER_N_KEYS = 128
PEER_N_EXPERTS = PEER_N_KEYS * PEER_N_KEYS
PEER_KEY_DIM = 256
PEER_TOPK = 16
PEER_TOK_CHUNK = 64
ATTN_Q_BLOCK = 128
N_BRANCH = 3
ALPHA = (2.0 * DEPTH) ** 0.25
BETA = (8.0 * DEPTH) ** -0.25
LN_EPS = 1e-5
NEG_INF = -1e30

kernel_name = 'hybrid_diffattn_moba_shortconv_peer_step'


def layer_norm(x, g, b):
    xf = x.astype(jnp.float32)
    mu = jnp.mean(xf, axis=-1, keepdims=True)
    var = jnp.mean(jnp.square(xf - mu), axis=-1, keepdims=True)
    return ((xf - mu) * lax.rsqrt(var + LN_EPS) * g.astype(jnp.float32) + b.astype(jnp.float32)).astype(x.dtype)


def split_inputs(u):
    B, T, _ = u.shape
    sizes = [DA_Q, DA_K, DA_V, MB_W, MB_W, MB_W, CONV_DIM, CONV_DIM, CONV_DIM]
    offs = np.cumsum(sizes)[:-1].tolist()
    q_a, k_a, v_a, q_b, k_b, v_b, xc, gb, gc = jnp.split(u, offs, axis=-1)
    q_a = q_a.reshape(B, T, DA_HEADS, 2, DA_HEAD_DIM)
    k_a = k_a.reshape(B, T, DA_KV_HEADS, 2 * DA_HEAD_DIM)
    v_a = v_a.reshape(B, T, DA_KV_HEADS, 2 * DA_HEAD_DIM)
    q_b = q_b.reshape(B, T, MB_HEADS, MB_HEAD_DIM)
    k_b = k_b.reshape(B, T, MB_HEADS, MB_HEAD_DIM)
    v_b = v_b.reshape(B, T, MB_HEADS, MB_HEAD_DIM)
    return q_a, k_a, v_a, q_b, k_b, v_b, xc, gb, gc


def gather_pages(cache, layer, page_table):
    rows = cache[layer, page_table]
    db, n_pages, page = rows.shape[:3]
    return rows.reshape(db, n_pages * page, *rows.shape[3:])


def diff_attn_block(q, k, v, q_pos, k_pos, lam, lam_init, subln_g):
    B, Tq = q.shape[:2]
    Tk = k.shape[1]
    G = DA_HEADS // DA_KV_HEADS
    qg = q.reshape(B, Tq, DA_KV_HEADS, G, 2, DA_HEAD_DIM)
    kk = k.reshape(B, Tk, DA_KV_HEADS, 2, DA_HEAD_DIM)
    s = jnp.einsum('bqkgcd,bskcd->bkgcqs', qg, kk).astype(jnp.float32) * (DA_HEAD_DIM ** -0.5)
    causal = k_pos[None, :] <= q_pos[:, None]
    s = jnp.where(causal, s, NEG_INF)
    p = jax.nn.softmax(s, axis=-1)
    a = p[:, :, :, 0] - lam * p[:, :, :, 1]
    o = jnp.einsum('bkgqs,bskd->bqkgd', a.astype(v.dtype), v).astype(jnp.float32)
    o = o * lax.rsqrt(jnp.mean(o * o, axis=-1, keepdims=True) + LN_EPS) * subln_g.astype(jnp.float32) * (1.0 - lam_init)
    return o.astype(q.dtype).reshape(B, Tq, DA_OUT)


def diff_attn_prompt(q, k, v, lam, lam_init, subln_g):
    B, S = q.shape[:2]
    nblk = S // ATTN_Q_BLOCK
    qb = q.reshape(B, nblk, ATTN_Q_BLOCK, DA_HEADS, 2, DA_HEAD_DIM).swapaxes(0, 1)
    pos = jnp.arange(S, dtype=jnp.int32)
    pb = pos.reshape(nblk, ATTN_Q_BLOCK)

    def one(args):
        qq, pp = args
        return diff_attn_block(qq, k, v, pp, pos, lam, lam_init, subln_g)

    o = lax.map(one, (qb, pb))
    return o.swapaxes(0, 1).reshape(B, S, DA_OUT)


def diff_attn_sample(q, k, v, lam, lam_init, subln_g, cache_k, cache_v, layer, page_table):
    k_all = jnp.concatenate([gather_pages(cache_k, layer, page_table).astype(k.dtype), k], axis=1)
    v_all = jnp.concatenate([gather_pages(cache_v, layer, page_table).astype(v.dtype), v], axis=1)
    T = q.shape[1]
    L = k_all.shape[1]
    q_pos = (L - T) + jnp.arange(T, dtype=jnp.int32)
    k_pos = jnp.arange(L, dtype=jnp.int32)
    return diff_attn_block(q, k_all, v_all, q_pos, k_pos, lam, lam_init, subln_g)


def moba_blocks(k, v):
    B, L = k.shape[:2]
    nb = -(-L // MB_BLOCK)
    pad = nb * MB_BLOCK - L
    kp = jnp.pad(k, ((0, 0), (0, pad), (0, 0), (0, 0)))
    vp = jnp.pad(v, ((0, 0), (0, pad), (0, 0), (0, 0)))
    kb = kp.reshape(B, nb, MB_BLOCK, MB_HEADS, MB_HEAD_DIM).transpose(0, 3, 1, 2, 4)
    vb = vp.reshape(B, nb, MB_BLOCK, MB_HEADS, MB_HEAD_DIM).transpose(0, 3, 1, 2, 4)
    kmean = jnp.mean(kb.astype(jnp.float32), axis=3)
    return kb, vb, kmean


def moba_block(q, q_pos, kb, vb, kmean):
    B, Qc = q.shape[:2]
    nb = kb.shape[2]
    ksel = min(MB_TOPK, nb)
    own = q_pos // MB_BLOCK
    gs = jnp.einsum('bqhd,bhnd->bqhn', q.astype(jnp.float32), kmean)
    fully_past = jnp.arange(nb, dtype=jnp.int32)[None, :] < own[:, None]
    gs = jnp.where(fully_past[None, :, None, :], gs, NEG_INF)
    _, sel = lax.top_k(gs, ksel)
    sel_ok = sel < own[None, :, None, None]
    own_b = jnp.broadcast_to(own[None, :, None, None], (B, Qc, MB_HEADS, 1)).astype(sel.dtype)
    idx = jnp.concatenate([sel, own_b], axis=-1)
    bi = jnp.arange(B)[:, None, None, None]
    hi = jnp.arange(MB_HEADS)[None, None, :, None]
    kg = kb[bi, hi, idx]
    vg = vb[bi, hi, idx]
    s = jnp.einsum('bqhd,bqhnjd->bqhnj', q, kg).astype(jnp.float32) * (MB_HEAD_DIM ** -0.5)
    kpos = idx[..., None] * MB_BLOCK + jnp.arange(MB_BLOCK, dtype=jnp.int32)
    slot_ok = jnp.concatenate([sel_ok, jnp.ones(own_b.shape, dtype=bool)], axis=-1)
    ok = slot_ok[..., None] & (kpos <= q_pos[None, :, None, None, None])
    s = jnp.where(ok, s, NEG_INF)
    p = jax.nn.softmax(s.reshape(B, Qc, MB_HEADS, -1), axis=-1).reshape(s.shape)
    o = jnp.einsum('bqhnj,bqhnjd->bqhd', p.astype(vg.dtype), vg)
    return o.reshape(B, Qc, MB_W)


def moba_prompt(q, k, v):
    B, S = q.shape[:2]
    kb, vb, km = moba_blocks(k, v)
    n = S // MB_Q_CHUNK
    qc = q.reshape(B, n, MB_Q_CHUNK, MB_HEADS, MB_HEAD_DIM).swapaxes(0, 1)
    pc = jnp.arange(S, dtype=jnp.int32).reshape(n, MB_Q_CHUNK)

    def one(args):
        qq, pp = args
        return moba_block(qq, pp, kb, vb, km)

    o = lax.map(one, (qc, pc))
    return o.swapaxes(0, 1).reshape(B, S, MB_W)


def moba_sample(q, k, v, cache_k, cache_v, layer, page_table):
    k_all = jnp.concatenate([gather_pages(cache_k, layer, page_table).astype(k.dtype), k], axis=1)
    v_all = jnp.concatenate([gather_pages(cache_v, layer, page_table).astype(v.dtype), v], axis=1)
    T = q.shape[1]
    L = k_all.shape[1]
    kb, vb, km = moba_blocks(k_all, v_all)
    q_pos = (L - T) + jnp.arange(T, dtype=jnp.int32)
    return moba_block(q, q_pos, kb, vb, km)


def short_conv(xc, gb, gc, conv_state, w):
    T = xc.shape[1]
    z = gc * xc
    zz = jnp.concatenate([conv_state.astype(z.dtype), z], axis=1)
    y = w[0] * zz[:, 0:T]
    for j in range(1, CONV_W):
        y = y + w[j] * zz[:, j:j + T]
    return gb * y, zz[:, T:]


def peer(x, w_q, sub_keys, u, v):
    B, T, D = x.shape
    n = B * T
    xt = x.reshape(n, D)
    q = (xt @ w_q).reshape(n, PEER_HEADS, 2, PEER_KEY_DIM // 2)
    s = jnp.einsum('nhcd,hckd->nhck', q, sub_keys).astype(jnp.float32)
    s1, i1 = lax.top_k(s[:, :, 0], PEER_TOPK)
    s2, i2 = lax.top_k(s[:, :, 1], PEER_TOPK)
    n_cand = PEER_TOPK * PEER_TOPK
    cand = (s1[..., :, None] + s2[..., None, :]).reshape(n, PEER_HEADS, n_cand)
    cand_idx = (i1[..., :, None] * PEER_N_KEYS + i2[..., None, :]).reshape(n, PEER_HEADS, n_cand)
    best, pos = lax.top_k(cand, PEER_TOPK)
    experts = jnp.take_along_axis(cand_idx, pos, axis=-1).reshape(n, PEER_HEADS * PEER_TOPK)
    gate = jax.nn.softmax(best, axis=-1).astype(x.dtype).reshape(n, PEER_HEADS * PEER_TOPK)
    c = min(PEER_TOK_CHUNK, n)
    pad = (-n) % c
    xt_p = jnp.pad(xt, ((0, pad), (0, 0))).reshape(-1, c, D)
    e_p = jnp.pad(experts, ((0, pad), (0, 0))).reshape(-1, c, PEER_HEADS * PEER_TOPK)
    g_p = jnp.pad(gate, ((0, pad), (0, 0))).reshape(-1, c, PEER_HEADS * PEER_TOPK)

    def one(args):
        xc, ec, gc = args
        h = jax.nn.gelu(jnp.einsum('cd,ced->ce', xc, u[ec]), approximate=False)
        return jnp.einsum('ce,ced->cd', gc * h, v[ec])

    out = lax.map(one, (xt_p, e_p, g_p))
    return out.reshape(-1, D)[:n].reshape(B, T, D)


def trunk_layer(x, conv_state, attend_a, attend_b, p, l):
    lam_init = 0.8 - 0.6 * math.exp(-0.3 * l)
    lp = p['da_lambda'][l].astype(jnp.float32)
    lam = jnp.exp(jnp.sum(lp[0] * lp[1])) - jnp.exp(jnp.sum(lp[2] * lp[3])) + lam_init
    u = x @ p['w_in'][l]
    q_a, k_a, v_a, q_b, k_b, v_b, xc, gb, gc = split_inputs(u)
    o_a = attend_a(q_a, k_a, v_a, lam, lam_init, p['da_subln_g'][l])
    o_b = attend_b(q_b, k_b, v_b)
    o_c, new_conv = short_conv(xc, gb, gc, conv_state, p['conv_w'][l])
    gates = jax.nn.sigmoid((x @ p['w_gate'][l] + p['b_gate'][l]).astype(jnp.float32)).astype(x.dtype)
    g_a, g_b, g_c = jnp.split(gates, N_BRANCH, axis=-1)
    merged = (g_a * (o_a @ p['w_br_a'][l]) + g_b * (o_b @ p['w_br_b'][l])
              + g_c * (o_c @ p['w_br_c'][l]))
    h = layer_norm(ALPHA * x + merged @ p['w_o'][l], p['ln1_g'][l], p['ln1_b'][l])
    f = peer(h, p['peer_wq'][l], p['peer_subkeys'][l], p['peer_u'][l], p['peer_v'][l])
    y = layer_norm(ALPHA * h + f, p['ln2_g'][l], p['ln2_b'][l])
    return y, (k_a, v_a, k_b, v_b, new_conv)


def setup_inputs(seed: int = 0) -> dict:
    key = jax.random.key(seed)
    ks = jax.random.split(key, 32)
    f32 = jnp.float32
    n_pages = PAST_LEN // PAGE_SIZE
    n_used = DEC_BATCH * n_pages
    n_pool = n_used + n_used // 4

    def nrm(k, shape, scale):
        return jax.random.normal(k, shape, f32) * scale

    col_scale = jnp.concatenate([
        jnp.ones((DA_Q + DA_K,), f32), jnp.full((DA_V,), BETA, f32),
        jnp.ones((2 * MB_W,), f32), jnp.full((MB_W,), BETA, f32),
        jnp.full((CONV_DIM,), BETA, f32), jnp.ones((2 * CONV_DIM,), f32)])
    page_table = jax.random.permutation(ks[7], n_pool)[:n_used].reshape(DEC_BATCH, n_pages).astype(jnp.int32)
    return {
        'x_prompt': nrm(ks[0], (BATCH, SEQ, D_MODEL), 1.0),
        'x_sample': nrm(ks[1], (DEC_BATCH, DEC_SEQ, D_MODEL), 1.0),
        'cache_diff_k': nrm(ks[2], (DEPTH, n_pool, PAGE_SIZE, DA_KV_HEADS, 2 * DA_HEAD_DIM), 1.0),
        'cache_diff_v': nrm(ks[3], (DEPTH, n_pool, PAGE_SIZE, DA_KV_HEADS, 2 * DA_HEAD_DIM), BETA),
        'cache_moba_k': nrm(ks[4], (DEPTH, n_pool, PAGE_SIZE, MB_HEADS, MB_HEAD_DIM), 1.0),
        'cache_moba_v': nrm(ks[5], (DEPTH, n_pool, PAGE_SIZE, MB_HEADS, MB_HEAD_DIM), BETA),
        'state_conv': nrm(ks[6], (DEPTH, DEC_BATCH, CONV_W - 1, CONV_DIM), 0.5),
        'page_table': page_table,
        'w_in': nrm(ks[8], (DEPTH, D_MODEL, IN_WIDTH), D_MODEL ** -0.5) * col_scale,
        'da_lambda': nrm(ks[9], (DEPTH, 4, DA_HEAD_DIM), 0.1),
        'da_subln_g': 1.0 + nrm(ks[10], (DEPTH, 2 * DA_HEAD_DIM), 0.02),
        'conv_w': nrm(ks[11], (DEPTH, CONV_W, CONV_DIM), CONV_W ** -0.5),
        'w_gate': nrm(ks[12], (DEPTH, D_MODEL, N_BRANCH * D_MODEL), D_MODEL ** -0.5),
        'b_gate': nrm(ks[13], (DEPTH, N_BRANCH * D_MODEL), 0.02),
        'w_br_a': nrm(ks[14], (DEPTH, DA_OUT, D_MODEL), DA_OUT ** -0.5),
        'w_br_b': nrm(ks[15], (DEPTH, MB_W, D_MODEL), MB_W ** -0.5),
        'w_br_c': nrm(ks[16], (DEPTH, CONV_DIM, D_MODEL), CONV_DIM ** -0.5),
        'w_o': nrm(ks[17], (DEPTH, D_MODEL, D_MODEL), BETA * D_MODEL ** -0.5),
        'ln1_g': 1.0 + nrm(ks[18], (DEPTH, D_MODEL), 0.02),
        'ln1_b': nrm(ks[19], (DEPTH, D_MODEL), 0.02),
        'peer_wq': nrm(ks[20], (DEPTH, D_MODEL, PEER_HEADS * PEER_KEY_DIM), D_MODEL ** -0.5),
        'peer_subkeys': nrm(ks[21], (DEPTH, PEER_HEADS, 2, PEER_N_KEYS, PEER_KEY_DIM // 2), (PEER_KEY_DIM // 2) ** -0.5),
        'peer_u': nrm(ks[22], (DEPTH, PEER_N_EXPERTS, D_MODEL), D_MODEL ** -0.5),
        'peer_v': nrm(ks[23], (DEPTH, PEER_N_EXPERTS, D_MODEL), BETA * PEER_HEADS ** -0.5),
        'ln2_g': 1.0 + nrm(ks[24], (DEPTH, D_MODEL), 0.02),
        'ln2_b': nrm(ks[25], (DEPTH, D_MODEL), 0.02),
    }


def reference(x_prompt, x_sample, cache_diff_k, cache_diff_v, cache_moba_k, cache_moba_v, state_conv, page_table,
              w_in, da_lambda, da_subln_g, conv_w, w_gate, b_gate, w_br_a, w_br_b, w_br_c, w_o, ln1_g, ln1_b,
              peer_wq, peer_subkeys, peer_u, peer_v, ln2_g, ln2_b):
    p = {'w_in': w_in, 'da_lambda': da_lambda, 'da_subln_g': da_subln_g, 'conv_w': conv_w,
         'w_gate': w_gate, 'b_gate': b_gate, 'w_br_a': w_br_a, 'w_br_b': w_br_b, 'w_br_c': w_br_c,
         'w_o': w_o, 'ln1_g': ln1_g, 'ln1_b': ln1_b, 'peer_wq': peer_wq, 'peer_subkeys': peer_subkeys,
         'peer_u': peer_u, 'peer_v': peer_v, 'ln2_g': ln2_g, 'ln2_b': ln2_b}
    x_p = x_prompt
    x_s = x_sample
    st_p = [[] for _ in range(5)]
    st_s = [[] for _ in range(5)]
    for l in range(DEPTH):
        zero_conv = jnp.zeros((x_p.shape[0], CONV_W - 1, CONV_DIM), x_p.dtype)
        x_p, new_p = trunk_layer(x_p, zero_conv, diff_attn_prompt, moba_prompt, p, l)
        att_a = functools.partial(diff_attn_sample, cache_k=cache_diff_k, cache_v=cache_diff_v,
                                  layer=l, page_table=page_table)
        att_b = functools.partial(moba_sample, cache_k=cache_moba_k, cache_v=cache_moba_v,
                                  layer=l, page_table=page_table)
        x_s, new_s = trunk_layer(x_s, state_conv[l], att_a, att_b, p, l)
        for i in range(5):
            st_p[i].append(new_p[i])
            st_s[i].append(new_s[i])
    new_diff_k_prompt = jnp.stack(st_p[0], axis=0)
    new_diff_v_prompt = jnp.stack(st_p[1], axis=0)
    new_moba_k_prompt = jnp.stack(st_p[2], axis=0)
    new_moba_v_prompt = jnp.stack(st_p[3], axis=0)
    new_conv_prompt = jnp.stack(st_p[4], axis=0)
    new_diff_k_sample = jnp.stack(st_s[0], axis=0)
    new_diff_v_sample = jnp.stack(st_s[1], axis=0)
    new_moba_k_sample = jnp.stack(st_s[2], axis=0)
    new_moba_v_sample = jnp.stack(st_s[3], axis=0)
    new_conv_sample = jnp.stack(st_s[4], axis=0)
    return (x_p, x_s, new_diff_k_prompt, new_diff_v_prompt, new_moba_k_prompt, new_moba_v_prompt, new_conv_prompt,
            new_diff_k_sample, new_diff_v_sample, new_moba_k_sample, new_moba_v_sample, new_conv_sample)
```

```python
import functools
import math

import jax
import jax.numpy as jnp
from jax import lax
from jax.experimental import pallas as pl
from jax.experimental.pallas import tpu as pltpu

F32 = jnp.float32
BF16 = jnp.bfloat16

D_MODEL = 4096
DEPTH = 2
PAGE_SIZE = 128
DA_HEADS = 8
DA_KV_HEADS = 4
DA_HEAD_DIM = 128
DA_Q = DA_HEADS * 2 * DA_HEAD_DIM
DA_K = DA_KV_HEADS * 2 * DA_HEAD_DIM
DA_V = DA_K
MB_HEADS = 8
MB_HEAD_DIM = 128
MB_W = MB_HEADS * MB_HEAD_DIM
MB_BLOCK = 256
MB_TOPK = 3
CONV_DIM = 1024
CONV_W = 3
IN_WIDTH = DA_Q + DA_K + DA_V + 3 * MB_W + 3 * CONV_DIM
PEER_HEADS = 8
PEER_N_KEYS = 128
PEER_N_EXPERTS = PEER_N_KEYS * PEER_N_KEYS
PEER_KEY_DIM = 256
PEER_TOPK = 16
ALPHA = (2.0 * DEPTH) ** 0.25
LN_EPS = 1e-5
NEG_INF = -1e30

OFF_QA = 0
OFF_KA = DA_Q
OFF_VA = OFF_KA + DA_K
OFF_QB = OFF_VA + DA_V
OFF_KB = OFF_QB + MB_W
OFF_VB = OFF_KB + MB_W
OFF_XC = OFF_VB + MB_W
OFF_GB = OFF_XC + CONV_DIM
OFF_GC = OFF_GB + CONV_DIM

VMEM_LIMIT = 56 * 1024 * 1024

_NT = (((1,), (1,)), ((), ()))
_TN = (((0,), (0,)), ((), ()))


def _cparams(sem):
    return pltpu.CompilerParams(dimension_semantics=sem, vmem_limit_bytes=VMEM_LIMIT)


def _mm_kernel(x_ref, w_ref, o_ref):
    o_ref[...] = jnp.dot(x_ref[...], w_ref[...], preferred_element_type=F32).astype(o_ref.dtype)


def matmul(x, w, out_dtype=F32, tm=1024, tn=512):
    m, k = x.shape
    n = w.shape[1]
    tm = min(tm, m)
    tn = min(tn, n)
    return pl.pallas_call(
        _mm_kernel,
        grid=(m // tm, n // tn),
        in_specs=[pl.BlockSpec((tm, k), lambda i, j: (i, 0)),
                  pl.BlockSpec((k, tn), lambda i, j: (0, j))],
        out_specs=pl.BlockSpec((tm, tn), lambda i, j: (i, j)),
        out_shape=jax.ShapeDtypeStruct((m, n), out_dtype),
        compiler_params=_cparams(("parallel", "arbitrary")),
        name="matmul",
    )(x, w)


def _merge_kernel(x_ref, oa_ref, ob_ref, oc_ref, wga_ref, wgb_ref, wgc_ref,
                  bga_ref, bgb_ref, bgc_ref, wa_ref, wb_ref, wc_ref, o_ref):
    x = x_ref[...]

    def branch(o_r, wg_r, bg_r, w_r):
        gate = jax.nn.sigmoid(jnp.dot(x, wg_r[...], preferred_element_type=F32) + bg_r[...])
        return gate * jnp.dot(o_r[...], w_r[...], preferred_element_type=F32)

    merged = (branch(oa_ref, wga_ref, bga_ref, wa_ref) + branch(ob_ref, wgb_ref, bgb_ref, wb_ref)
              + branch(oc_ref, wgc_ref, bgc_ref, wc_ref))
    o_ref[...] = merged.astype(o_ref.dtype)


def gated_merge(x, o_a, o_b, o_c, w_gate, b_gate, w_a, w_b, w_c, tm=512, tn=256):
    m, d = x.shape
    tm = min(tm, m)
    nj = d // tn
    row = lambda i, j: (i, 0)
    return pl.pallas_call(
        _merge_kernel,
        grid=(m // tm, nj),
        in_specs=[pl.BlockSpec((tm, d), row),
                  pl.BlockSpec((tm, o_a.shape[1]), row),
                  pl.BlockSpec((tm, o_b.shape[1]), row),
                  pl.BlockSpec((tm, o_c.shape[1]), row),
                  pl.BlockSpec((d, tn), lambda i, j: (0, j)),
                  pl.BlockSpec((d, tn), lambda i, j: (0, j + nj)),
                  pl.BlockSpec((d, tn), lambda i, j: (0, j + 2 * nj)),
                  pl.BlockSpec((1, tn), lambda i, j: (0, j)),
                  pl.BlockSpec((1, tn), lambda i, j: (0, j + nj)),
                  pl.BlockSpec((1, tn), lambda i, j: (0, j + 2 * nj)),
                  pl.BlockSpec((w_a.shape[0], tn), lambda i, j: (0, j)),
                  pl.BlockSpec((w_b.shape[0], tn), lambda i, j: (0, j)),
                  pl.BlockSpec((w_c.shape[0], tn), lambda i, j: (0, j))],
        out_specs=pl.BlockSpec((tm, tn), lambda i, j: (i, j)),
        out_shape=jax.ShapeDtypeStruct((m, d), BF16),
        compiler_params=_cparams(("parallel", "arbitrary")),
        name="gated_merge",
    )(x, o_a, o_b, o_c, w_gate, w_gate, w_gate, b_gate, b_gate, b_gate, w_a, w_b, w_c)


def _layer_norm(z, g, b):
    mu = jnp.mean(z, axis=-1, keepdims=True)
    zc = z - mu
    var = jnp.mean(zc * zc, axis=-1, keepdims=True)
    return zc * lax.rsqrt(var + LN_EPS) * g + b


def _mm_ln_kernel(a_ref, w_ref, res_ref, g_ref, b_ref, o_ref, obf_ref, acc_ref):
    k = pl.program_id(1)

    @pl.when(k == 0)
    def _():
        acc_ref[...] = jnp.zeros_like(acc_ref)

    acc_ref[...] += jnp.dot(a_ref[...], w_ref[...], preferred_element_type=F32)

    @pl.when(k == pl.num_programs(1) - 1)
    def _():
        y = _layer_norm(ALPHA * res_ref[...] + acc_ref[...], g_ref[...], b_ref[...])
        o_ref[...] = y
        obf_ref[...] = y.astype(BF16)


def matmul_residual_ln(a, w, res, g, b, tm=256, tk=512):
    m, kdim = a.shape
    d = w.shape[1]
    tm = min(tm, m)
    return pl.pallas_call(
        _mm_ln_kernel,
        grid=(m // tm, kdim // tk),
        in_specs=[pl.BlockSpec((tm, tk), lambda i, k: (i, k)),
                  pl.BlockSpec((tk, d), lambda i, k: (k, 0)),
                  pl.BlockSpec((tm, d), lambda i, k: (i, 0)),
                  pl.BlockSpec((1, d), lambda i, k: (0, 0)),
                  pl.BlockSpec((1, d), lambda i, k: (0, 0))],
        out_specs=[pl.BlockSpec((tm, d), lambda i, k: (i, 0)),
                   pl.BlockSpec((tm, d), lambda i, k: (i, 0))],
        out_shape=[jax.ShapeDtypeStruct((m, d), F32), jax.ShapeDtypeStruct((m, d), BF16)],
        scratch_shapes=[pltpu.VMEM((tm, d), F32)],
        compiler_params=_cparams(("parallel", "arbitrary")),
        name="matmul_residual_ln",
    )(a, w, res, g, b)


def _add_ln_kernel(res_ref, f_ref, g_ref, b_ref, o_ref, obf_ref):
    y = _layer_norm(ALPHA * res_ref[...] + f_ref[...], g_ref[...], b_ref[...])
    o_ref[...] = y
    obf_ref[...] = y.astype(BF16)


def add_ln(res, f, g, b, tm=256):
    m, d = res.shape
    tm = min(tm, m)
    spec = pl.BlockSpec((tm, d), lambda i: (i, 0))
    vec = pl.BlockSpec((1, d), lambda i: (0, 0))
    return pl.pallas_call(
        _add_ln_kernel,
        grid=(m // tm,),
        in_specs=[spec, spec, vec, vec],
        out_specs=[spec, spec],
        out_shape=[jax.ShapeDtypeStruct((m, d), F32), jax.ShapeDtypeStruct((m, d), BF16)],
        compiler_params=_cparams(("parallel",)),
        name="add_ln",
    )(res, f, g, b)


def _lambda_value(lp, lam_init):
    a = jnp.sum(lp[0:1, :] * lp[1:2, :], axis=-1, keepdims=True)
    b = jnp.sum(lp[2:3, :] * lp[3:4, :], axis=-1, keepdims=True)
    return jnp.exp(a) - jnp.exp(b) + lam_init


def _online_update(s, v_bf, m_ref, l_ref, acc_ref):
    m_prev = m_ref[...]
    m_new = jnp.maximum(m_prev, jnp.max(s, axis=-1, keepdims=True))
    alpha = jnp.exp(m_prev - m_new)
    p = jnp.exp(s - m_new)
    l_ref[...] = alpha * l_ref[...] + jnp.sum(p, axis=-1, keepdims=True)
    acc_ref[...] = alpha * acc_ref[...] + jnp.dot(p.astype(BF16), v_bf, preferred_element_type=F32)
    m_ref[...] = m_new


def _subln(o, g, lam_init):
    return o * lax.rsqrt(jnp.mean(o * o, axis=-1, keepdims=True) + LN_EPS) * g * (1.0 - lam_init)


def _diff_prompt_kernel(q_ref, k_ref, v_ref, lam_ref, g_ref, o_ref, m_ref, l_ref, acc_ref, *, lam_init, tq):
    i = pl.program_id(2)
    dh = DA_HEAD_DIM
    scale = dh ** -0.5
    lam = _lambda_value(lam_ref[...], lam_init)
    row = lax.broadcasted_iota(jnp.int32, (tq, tq), 0)
    col = lax.broadcasted_iota(jnp.int32, (tq, tq), 1)
    causal = col <= row

    for gi in range(2):
        outs = []
        for c in range(2):
            q = q_ref[:, (gi * 2 + c) * dh:(gi * 2 + c + 1) * dh].astype(BF16)
            m_ref[...] = jnp.full_like(m_ref, NEG_INF)
            l_ref[...] = jnp.zeros_like(l_ref)
            acc_ref[...] = jnp.zeros_like(acc_ref)

            def tile(j, masked):
                start = pl.multiple_of(j * tq, tq)
                kt = k_ref[pl.ds(start, tq), c * dh:(c + 1) * dh].astype(BF16)
                vt = v_ref[pl.ds(start, tq), :].astype(BF16)
                s = lax.dot_general(q, kt, _NT, preferred_element_type=F32) * scale
                if masked:
                    s = jnp.where(causal, s, NEG_INF)
                _online_update(s, vt, m_ref, l_ref, acc_ref)

            def body(j, carry):
                tile(j, False)
                return carry

            lax.fori_loop(0, i, body, 0)
            tile(i, True)
            outs.append(acc_ref[...] / l_ref[...])
        o = _subln(outs[0] - lam * outs[1], g_ref[...], lam_init)
        o_ref[:, gi * 2 * dh:(gi + 1) * 2 * dh] = o.astype(o_ref.dtype)


def diff_attn_prompt(u, lam_p, subln_g, batch, seq, lam_init, tq=256):
    nq = seq // tq
    qw = 4 * DA_HEAD_DIM
    kw = 2 * DA_HEAD_DIM
    kern = functools.partial(_diff_prompt_kernel, lam_init=lam_init, tq=tq)
    return pl.pallas_call(
        kern,
        grid=(batch, DA_KV_HEADS, nq),
        in_specs=[pl.BlockSpec((tq, qw), lambda b, k, i: (b * nq + i, k)),
                  pl.BlockSpec((seq, kw), lambda b, k, i: (b, OFF_KA // kw + k)),
                  pl.BlockSpec((seq, kw), lambda b, k, i: (b, OFF_VA // kw + k)),
                  pl.BlockSpec((4, DA_HEAD_DIM), lambda b, k, i: (0, 0)),
                  pl.BlockSpec((1, kw), lambda b, k, i: (0, 0))],
        out_specs=pl.BlockSpec((tq, qw), lambda b, k, i: (b * nq + i, k)),
        out_shape=jax.ShapeDtypeStruct((batch * seq, DA_Q), BF16),
        scratch_shapes=[pltpu.VMEM((tq, 1), F32), pltpu.VMEM((tq, 1), F32), pltpu.VMEM((tq, kw), F32)],
        compiler_params=_cparams(("parallel", "parallel", "arbitrary")),
        name="diff_attn_prompt",
    )(u, u, u, lam_p, subln_g)


def _moba_prompt_kernel(q_ref, k_ref, v_ref, o_ref, km_ref, sel_ref, m_ref, l_ref, acc_ref, *, nb):
    i = pl.program_id(2)
    blk = MB_BLOCK
    scale = MB_HEAD_DIM ** -0.5
    qf = q_ref[...]
    q = qf.astype(BF16)

    km_ref[...] = jnp.zeros_like(km_ref)
    for n in range(nb):
        km_ref[n:n + 1, :] = jnp.mean(k_ref[n * blk:(n + 1) * blk, :], axis=0, keepdims=True)
    gs = lax.dot_general(qf, km_ref[...], _NT, preferred_element_type=F32,
                         precision=lax.Precision.HIGHEST)
    lane = lax.broadcasted_iota(jnp.int32, gs.shape, 1)
    past = lane < i
    for n in range(nb):
        gn = gs[:, n:n + 1]
        ahead = (gs > gn) | ((gs == gn) & (lane < n))
        rank = jnp.sum(jnp.where(past & ahead, 1.0, 0.0), axis=-1, keepdims=True)
        sel_ref[n] = jnp.where(rank < MB_TOPK, 1.0, 0.0)

    row = lax.broadcasted_iota(jnp.int32, (blk, blk), 0)
    col = lax.broadcasted_iota(jnp.int32, (blk, blk), 1)
    m_ref[...] = jnp.full_like(m_ref, NEG_INF)
    l_ref[...] = jnp.zeros_like(l_ref)
    acc_ref[...] = jnp.zeros_like(acc_ref)

    def tile(n, mask):
        start = pl.multiple_of(n * blk, blk)
        kt = k_ref[pl.ds(start, blk), :].astype(BF16)
        vt = v_ref[pl.ds(start, blk), :].astype(BF16)
        s = lax.dot_general(q, kt, _NT, preferred_element_type=F32) * scale
        s = jnp.where(mask, s, NEG_INF)
        _online_update(s, vt, m_ref, l_ref, acc_ref)

    tile(i, col <= row)

    def body(n, carry):
        tile(n, sel_ref[n] > 0.5)
        return carry

    lax.fori_loop(0, i, body, 0)
    o_ref[...] = (acc_ref[...] / l_ref[...]).astype(o_ref.dtype)


def moba_prompt(u, batch, seq):
    nb = seq // MB_BLOCK
    dh = MB_HEAD_DIM
    kern = functools.partial(_moba_prompt_kernel, nb=nb)
    return pl.pallas_call(
        kern,
        grid=(batch, MB_HEADS, nb),
        in_specs=[pl.BlockSpec((MB_BLOCK, dh), lambda b, h, i: (b * nb + i, OFF_QB // dh + h)),
                  pl.BlockSpec((seq, dh), lambda b, h, i: (b, OFF_KB // dh + h)),
                  pl.BlockSpec((seq, dh), lambda b, h, i: (b, OFF_VB // dh + h))],
        out_specs=pl.BlockSpec((MB_BLOCK, dh), lambda b, h, i: (b * nb + i, h)),
        out_shape=jax.ShapeDtypeStruct((batch * seq, MB_W), BF16),
        scratch_shapes=[pltpu.VMEM((128, dh), F32), pltpu.VMEM((nb, MB_BLOCK, 1), F32),
                        pltpu.VMEM((MB_BLOCK, 1), F32), pltpu.VMEM((MB_BLOCK, 1), F32),
                        pltpu.VMEM((MB_BLOCK, dh), F32)],
        compiler_params=_cparams(("parallel", "parallel", "arbitrary")),
        name="moba_prompt",
    )(u, u, u)


def _conv_prompt_kernel(xc_ref, gb_ref, gc_ref, w_ref, o_ref, st_ref):
    z = gc_ref[...] * xc_ref[...]
    t = z.shape[0]
    row = lax.broadcasted_iota(jnp.int32, z.shape, 0)
    z1 = jnp.where(row >= 1, pltpu.roll(z, 1, 0), 0.0)
    z2 = jnp.where(row >= 2, pltpu.roll(z, 2, 0), 0.0)
    y = w_ref[0:1, :] * z2 + w_ref[1:2, :] * z1 + w_ref[2:3, :] * z
    o_ref[...] = (gb_ref[...] * y).astype(o_ref.dtype)
    st_ref[0] = z[t - 8:, :]


def conv_prompt(u, conv_w, batch, seq, tc=256):
    nc = CONV_DIM // tc
    blk = lambda off: pl.BlockSpec((seq, tc), lambda b, c: (b, off // tc + c))
    return pl.pallas_call(
        _conv_prompt_kernel,
        grid=(batch, nc),
        in_specs=[blk(OFF_XC), blk(OFF_GB), blk(OFF_GC), pl.BlockSpec((CONV_W, tc), lambda b, c: (0, c))],
        out_specs=[pl.BlockSpec((seq, tc), lambda b, c: (b, c)),
                   pl.BlockSpec((1, 8, tc), lambda b, c: (b, 0, c))],
        out_shape=[jax.ShapeDtypeStruct((batch * seq, CONV_DIM), BF16),
                   jax.ShapeDtypeStruct((batch, 8, CONV_DIM), F32)],
        compiler_params=_cparams(("parallel", "parallel")),
        name="conv_prompt",
    )(u, u, u, conv_w)


def _conv_sample_kernel(xc_ref, gb_ref, gc_ref, st_ref, w_ref, o_ref, nst_ref):
    t_len = xc_ref.shape[0]
    zz = [st_ref[0], st_ref[1]] + [gc_ref[t] * xc_ref[t] for t in range(t_len)]
    for t in range(t_len):
        y = w_ref[0:1, :] * zz[t] + w_ref[1:2, :] * zz[t + 1] + w_ref[2:3, :] * zz[t + 2]
        o_ref[t] = gb_ref[t] * y
    nst_ref[0] = zz[t_len]
    nst_ref[1] = zz[t_len + 1]


def conv_sample(xc, gb, gc, state, conv_w):
    return pl.pallas_call(
        _conv_sample_kernel,
        out_shape=[jax.ShapeDtypeStruct(xc.shape, F32), jax.ShapeDtypeStruct(state.shape, F32)],
        name="conv_sample",
    )(xc, gb, gc, state, conv_w)


def _softmax_rows_update(s, v_bf, m_ref, l_ref, acc_ref):
    _online_update(s, v_bf, m_ref, l_ref, acc_ref)


def _new_key_mask(rows, t_len):
    r = lax.broadcasted_iota(jnp.int32, (rows, 8), 0)
    j = lax.broadcasted_iota(jnp.int32, (rows, 8), 1)
    return j <= (r % t_len)


def _diff_sample_kernel(pt_ref, q_ref, k_ref, v_ref, kn_ref, vn_ref, lam_ref, g_ref, o_ref,
                        m_ref, l_ref, acc_ref, *, lam_init, t_len):
    p = pl.program_id(1)
    scale = DA_HEAD_DIM ** -0.5
    q = q_ref[0].astype(BF16)

    @pl.when(p == 0)
    def _():
        m_ref[...] = jnp.full_like(m_ref, NEG_INF)
        l_ref[...] = jnp.zeros_like(l_ref)
        acc_ref[...] = jnp.zeros_like(acc_ref)

    s = lax.dot_general(q, k_ref[...].astype(BF16), _NT, preferred_element_type=F32) * scale
    _online_update(s, v_ref[...].astype(BF16), m_ref, l_ref, acc_ref)

    @pl.when(p == pl.num_programs(1) - 1)
    def _():
        sn = lax.dot_general(q, kn_ref[0].astype(BF16), _NT, preferred_element_type=F32) * scale
        sn = jnp.where(_new_key_mask(sn.shape[0], t_len), sn, NEG_INF)
        _online_update(sn, vn_ref[0].astype(BF16), m_ref, l_ref, acc_ref)
        o = acc_ref[...] / l_ref[...]
        lam = _lambda_value(lam_ref[...], lam_init)
        w = 2 * DA_HEAD_DIM
        rows = 2 * t_len
        for k in range(DA_KV_HEADS):
            base = k * 2 * rows
            o1 = o[base:base + rows, k * w:(k + 1) * w]
            o2 = o[base + rows:base + 2 * rows, k * w:(k + 1) * w]
            o_ref[0, k] = _subln(o1 - lam * o2, g_ref[...], lam_init)


def diff_attn_sample(qbd, cache_k, cache_v, k_new, v_new, page_table, lam_p, subln_g, layer, lam_init, t_len):
    db, n_pages = page_table.shape
    w = DA_K
    rows = 2 * t_len
    kern = functools.partial(_diff_sample_kernel, lam_init=lam_init, t_len=t_len)
    page = pl.BlockSpec((None, None, PAGE_SIZE, w), lambda b, p, pt: (layer, pt[b, p], 0, 0))
    per_b = lambda shape: pl.BlockSpec(shape, lambda b, p, pt: (b, 0, 0))
    grid_spec = pltpu.PrefetchScalarGridSpec(
        num_scalar_prefetch=1,
        grid=(db, n_pages),
        in_specs=[per_b((1, 128, w)), page, page, per_b((1, 8, w)), per_b((1, 8, w)),
                  pl.BlockSpec((4, DA_HEAD_DIM), lambda b, p, pt: (0, 0)),
                  pl.BlockSpec((1, 2 * DA_HEAD_DIM), lambda b, p, pt: (0, 0))],
        out_specs=pl.BlockSpec((1, DA_KV_HEADS, rows, 2 * DA_HEAD_DIM), lambda b, p, pt: (b, 0, 0, 0)),
        scratch_shapes=[pltpu.VMEM((128, 1), F32), pltpu.VMEM((128, 1), F32), pltpu.VMEM((128, w), F32)],
    )
    return pl.pallas_call(
        kern,
        grid_spec=grid_spec,
        out_shape=jax.ShapeDtypeStruct((db, DA_KV_HEADS, rows, 2 * DA_HEAD_DIM), F32),
        compiler_params=_cparams(("parallel", "arbitrary")),
        name="diff_attn_sample",
    )(page_table, qbd, cache_k, cache_v, k_new, v_new, lam_p, subln_g)


def _moba_gate_kernel(pt_ref, q_ref, k_ref, gs_ref, cs_ref):
    p = pl.program_id(1)
    pages_per_block = MB_BLOCK // PAGE_SIZE

    @pl.when(p == 0)
    def _():
        gs_ref[...] = jnp.zeros_like(gs_ref)

    @pl.when(p % pages_per_block == 0)
    def _():
        cs_ref[...] = jnp.zeros_like(cs_ref)

    cs_ref[...] += jnp.sum(k_ref[...], axis=0, keepdims=True)

    @pl.when(p % pages_per_block == pages_per_block - 1)
    def _():
        km = cs_ref[...] / MB_BLOCK
        g = jnp.sum(q_ref[0] * km, axis=-1, keepdims=True)
        lane = lax.broadcasted_iota(jnp.int32, gs_ref.shape[1:], 1)
        gs_ref[0] += jnp.where(lane == p // pages_per_block, g, 0.0)


def moba_gate_sample(qbd, cache_k, page_table, layer):
    db, n_pages = page_table.shape
    grid_spec = pltpu.PrefetchScalarGridSpec(
        num_scalar_prefetch=1,
        grid=(db, n_pages),
        in_specs=[pl.BlockSpec((1, 128, MB_W), lambda b, p, pt: (b, 0, 0)),
                  pl.BlockSpec((None, None, PAGE_SIZE, MB_W), lambda b, p, pt: (layer, pt[b, p], 0, 0))],
        out_specs=pl.BlockSpec((1, 128, 128), lambda b, p, pt: (b, 0, 0)),
        scratch_shapes=[pltpu.VMEM((1, MB_W), F32)],
    )
    return pl.pallas_call(
        _moba_gate_kernel,
        grid_spec=grid_spec,
        out_shape=jax.ShapeDtypeStruct((db, 128, 128), F32),
        compiler_params=_cparams(("parallel", "arbitrary")),
        name="moba_gate_sample",
    )(page_table, qbd, cache_k)


def _moba_sample_kernel(pt_ref, q_ref, gs_ref, k_ref, v_ref, kn_ref, vn_ref, o_ref,
                        sel_ref, m_ref, l_ref, acc_ref, *, t_len, n_blocks):
    p = pl.program_id(1)
    scale = MB_HEAD_DIM ** -0.5
    pages_per_block = MB_BLOCK // PAGE_SIZE
    q = q_ref[0].astype(BF16)
    lane = lax.broadcasted_iota(jnp.int32, (128, 128), 1).astype(F32)

    @pl.when(p == 0)
    def _():
        cur = jnp.where(lane < n_blocks, gs_ref[0], -jnp.inf)
        sel = jnp.zeros((128, 128), F32)
        for _ in range(MB_TOPK):
            mx = jnp.max(cur, axis=-1, keepdims=True)
            first = jnp.min(jnp.where(cur == mx, lane, 128.0), axis=-1, keepdims=True)
            hit = lane == first
            sel = jnp.where(hit, 1.0, sel)
            cur = jnp.where(hit, -jnp.inf, cur)
        sel_ref[...] = sel
        m_ref[...] = jnp.full_like(m_ref, NEG_INF)
        l_ref[...] = jnp.zeros_like(l_ref)
        acc_ref[...] = jnp.zeros_like(acc_ref)
        sn = lax.dot_general(q, kn_ref[0].astype(BF16), _NT, preferred_element_type=F32) * scale
        sn = jnp.where(_new_key_mask(sn.shape[0], t_len), sn, NEG_INF)
        _online_update(sn, vn_ref[0].astype(BF16), m_ref, l_ref, acc_ref)

    block = (p // pages_per_block).astype(F32)
    chosen = jnp.sum(jnp.where(lane == block, sel_ref[...], 0.0), axis=-1, keepdims=True)
    s = lax.dot_general(q, k_ref[...].astype(BF16), _NT, preferred_element_type=F32) * scale
    s = jnp.where(chosen > 0.5, s, NEG_INF)
    _online_update(s, v_ref[...].astype(BF16), m_ref, l_ref, acc_ref)

    @pl.when(p == pl.num_programs(1) - 1)
    def _():
        o_ref[0] = acc_ref[...] / l_ref[...]


def moba_attn_sample(qbd, gs, cache_k, cache_v, k_new, v_new, page_table, layer, t_len):
    db, n_pages = page_table.shape
    n_blocks = n_pages * PAGE_SIZE // MB_BLOCK
    kern = functools.partial(_moba_sample_kernel, t_len=t_len, n_blocks=n_blocks)
    page = pl.BlockSpec((None, None, PAGE_SIZE, MB_W), lambda b, p, pt: (layer, pt[b, p], 0, 0))
    per_b = lambda shape: pl.BlockSpec(shape, lambda b, p, pt: (b, 0, 0))
    grid_spec = pltpu.PrefetchScalarGridSpec(
        num_scalar_prefetch=1,
        grid=(db, n_pages),
        in_specs=[per_b((1, 128, MB_W)), per_b((1, 128, 128)), page, page,
                  per_b((1, 8, MB_W)), per_b((1, 8, MB_W))],
        out_specs=per_b((1, 128, MB_W)),
        scratch_shapes=[pltpu.VMEM((128, 128), F32), pltpu.VMEM((128, 1), F32), pltpu.VMEM((128, 1), F32),
                        pltpu.VMEM((128, MB_W), F32)],
    )
    return pl.pallas_call(
        kern,
        grid_spec=grid_spec,
        out_shape=jax.ShapeDtypeStruct((db, 128, MB_W), F32),
        compiler_params=_cparams(("parallel", "arbitrary")),
        name="moba_attn_sample",
    )(page_table, qbd, gs, cache_k, cache_v, k_new, v_new)


def _topk_rows(cur_ref, out_ref, k):
    def body(j, carry):
        cur = cur_ref[...]
        mx = jnp.max(cur, axis=0, keepdims=True)
        out_ref[pl.ds(j, 1), :] = mx
        cur_ref[...] = jnp.where(cur == mx, -jnp.inf, cur)
        return carry

    lax.fori_loop(0, k, body, 0)


def _peer_route_kernel(q_ref, sk_ref, s1m_ref, s2m_ref, eb_ref, a0_ref, thr_ref,
                       cur_ref, top_a_ref, top_b_ref, cand_ref, best_ref):
    nk = PEER_N_KEYS
    half = PEER_KEY_DIM // 2
    for h in range(PEER_HEADS):
        tops = (top_a_ref, top_b_ref)
        scores = []
        for c in range(2):
            qc = q_ref[:, (h * 2 + c) * half:(h * 2 + c + 1) * half]
            s = lax.dot_general(sk_ref[h, c], qc, _NT, preferred_element_type=F32,
                                precision=lax.Precision.HIGHEST)
            scores.append(s)
            cur_ref[...] = s
            _topk_rows(cur_ref, tops[c], PEER_TOPK)
        ta = top_a_ref[...]
        tb = top_b_ref[...]
        for pidx in range(PEER_TOPK):
            cand_ref[pidx * PEER_TOPK:(pidx + 1) * PEER_TOPK, :] = ta[pidx:pidx + 1, :] + tb
        _topk_rows(cand_ref, best_ref, PEER_TOPK)
        best = best_ref[...]
        z = jnp.sum(jnp.exp(best - best[0:1, :]), axis=0, keepdims=True)
        s1, s2 = scores
        s1m_ref[h] = jnp.where(s1 >= ta[PEER_TOPK - 1:PEER_TOPK, :], s1, NEG_INF)
        s2m_ref[h] = jnp.where(s2 >= tb[PEER_TOPK - 1:PEER_TOPK, :], s2, NEG_INF)
        eb_ref[h] = jnp.exp(s2 - tb[0:1, :]) / z
        a0_ref[h:h + 1, :] = ta[0:1, :]
        thr_ref[h:h + 1, :] = best[PEER_TOPK - 1:PEER_TOPK, :]


def peer_route(q, sub_keys, tm=256):
    m = q.shape[0]
    tm = min(tm, m)
    nk = PEER_N_KEYS
    big = pl.BlockSpec((PEER_HEADS, nk, tm), lambda i: (0, 0, i))
    small = pl.BlockSpec((PEER_HEADS, tm), lambda i: (0, i))
    big_shape = jax.ShapeDtypeStruct((PEER_HEADS, nk, m), F32)
    small_shape = jax.ShapeDtypeStruct((PEER_HEADS, m), F32)
    return pl.pallas_call(
        _peer_route_kernel,
        grid=(m // tm,),
        in_specs=[pl.BlockSpec((tm, PEER_HEADS * PEER_KEY_DIM), lambda i: (i, 0)),
                  pl.BlockSpec(sub_keys.shape, lambda i: (0, 0, 0, 0))],
        out_specs=[big, big, big, small, small],
        out_shape=[big_shape, big_shape, big_shape, small_shape, small_shape],
        scratch_shapes=[pltpu.VMEM((nk, tm), F32), pltpu.VMEM((PEER_TOPK, tm), F32),
                        pltpu.VMEM((PEER_TOPK, tm), F32), pltpu.VMEM((PEER_TOPK * PEER_TOPK, tm), F32),
                        pltpu.VMEM((PEER_TOPK, tm), F32)],
        compiler_params=_cparams(("parallel",)),
        name="peer_route",
    )(q, sub_keys)


def _peer_expert_kernel(x_ref, u_ref, v_ref, s1m_ref, s2m_ref, eb_ref, a0_ref, thr_ref, o_ref, a_ref, *, te):
    e = pl.program_id(1)
    nk = PEER_N_KEYS

    @pl.when(e == 0)
    def _():
        o_ref[...] = jnp.zeros_like(o_ref)

    g = lax.dot_general(u_ref[...], x_ref[...], _NT, preferred_element_type=F32)
    for ii in range(te // nk):
        i = e * (te // nk) + ii
        w = jnp.zeros((nk, g.shape[1]), F32)
        for h in range(PEER_HEADS):
            s1 = s1m_ref[h, pl.ds(i, 1), :]
            ea = jnp.exp(s1 - a0_ref[h:h + 1, :])
            hit = (s1 + s2m_ref[h]) >= thr_ref[h:h + 1, :]
            w = w + jnp.where(hit, ea * eb_ref[h], 0.0)
        gi = g[ii * nk:(ii + 1) * nk, :]
        act = 0.5 * gi * (1.0 + lax.erf(gi * (2.0 ** -0.5)))
        a_ref[ii * nk:(ii + 1) * nk, :] = (w * act).astype(BF16)
    o_ref[...] += lax.dot_general(a_ref[...], v_ref[...], _TN, preferred_element_type=F32)


def peer_experts(x, u, v, s1m, s2m, eb, a0, thr, tm=512, te=256):
    m, d = x.shape
    tm = min(tm, m)
    n_e = u.shape[0]
    kern = functools.partial(_peer_expert_kernel, te=te)
    big = pl.BlockSpec((PEER_HEADS, PEER_N_KEYS, tm), lambda i, e: (0, 0, i))
    small = pl.BlockSpec((PEER_HEADS, tm), lambda i, e: (0, i))
    return pl.pallas_call(
        kern,
        grid=(m // tm, n_e // te),
        in_specs=[pl.BlockSpec((tm, d), lambda i, e: (i, 0)),
                  pl.BlockSpec((te, d), lambda i, e: (e, 0)),
                  pl.BlockSpec((te, d), lambda i, e: (e, 0)),
                  big, big, big, small, small],
        out_specs=pl.BlockSpec((tm, d), lambda i, e: (i, 0)),
        out_shape=jax.ShapeDtypeStruct((m, d), F32),
        scratch_shapes=[pltpu.VMEM((te, tm), BF16)],
        compiler_params=_cparams(("parallel", "arbitrary")),
        name="peer_experts",
    )(x, u, v, s1m, s2m, eb, a0, thr)


def _token_stage(x, x_bf, o_a, o_b, o_c, wts):
    merged = gated_merge(x_bf, o_a, o_b, o_c, wts['w_gate'], wts['b_gate'], wts['w_br_a'], wts['w_br_b'],
                         wts['w_br_c'])
    h, h_bf = matmul_residual_ln(merged, wts['w_o'], x, wts['ln1_g'], wts['ln1_b'])
    q = matmul(h_bf, wts['peer_wq'])
    s1m, s2m, eb, a0, thr = peer_route(q, wts['peer_subkeys'])
    f = peer_experts(h_bf, wts['peer_u'], wts['peer_v'], s1m, s2m, eb, a0, thr)
    return add_ln(h, f, wts['ln2_g'], wts['ln2_b'])


def _block_diag_queries(q, rows):
    db, r, g, dh = q.shape
    eye = jnp.eye(g, dtype=q.dtype)
    qb = jnp.einsum('brgd,gk->bgrkd', q, eye).reshape(db, g * r, g * dh)
    return jnp.pad(qb, ((0, 0), (0, rows - g * r), (0, 0)))


def kernel(x_prompt, x_sample, cache_diff_k, cache_diff_v, cache_moba_k, cache_moba_v, state_conv, page_table,
           w_in, da_lambda, da_subln_g, conv_w, w_gate, b_gate, w_br_a, w_br_b, w_br_c, w_o, ln1_g, ln1_b,
           peer_wq, peer_subkeys, peer_u, peer_v, ln2_g, ln2_b):
    batch, seq, d = x_prompt.shape
    db, t_len, _ = x_sample.shape
    n_p = batch * seq
    n_s = db * t_len
    pad_s = 128
    n_pool = cache_diff_k.shape[1]

    ck_a = cache_diff_k.reshape(DEPTH, n_pool, PAGE_SIZE, DA_K)
    cv_a = cache_diff_v.reshape(DEPTH, n_pool, PAGE_SIZE, DA_V)
    ck_b = cache_moba_k.reshape(DEPTH, n_pool, PAGE_SIZE, MB_W)
    cv_b = cache_moba_v.reshape(DEPTH, n_pool, PAGE_SIZE, MB_W)

    xp = x_prompt.reshape(n_p, d)
    xs = jnp.pad(x_sample.reshape(n_s, d), ((0, pad_s - n_s), (0, 0)))
    xp_bf = xp.astype(BF16)
    xs_bf = xs.astype(BF16)

    st_p = [[] for _ in range(5)]
    st_s = [[] for _ in range(5)]
    for l in range(DEPTH):
        lam_init = 0.8 - 0.6 * math.exp(-0.3 * l)
        wts = {
            'w_gate': w_gate[l].astype(BF16), 'b_gate': b_gate[l].reshape(1, -1),
            'w_br_a': w_br_a[l].astype(BF16), 'w_br_b': w_br_b[l].astype(BF16), 'w_br_c': w_br_c[l].astype(BF16),
            'w_o': w_o[l].astype(BF16), 'ln1_g': ln1_g[l].reshape(1, -1), 'ln1_b': ln1_b[l].reshape(1, -1),
            'peer_wq': peer_wq[l].astype(BF16), 'peer_subkeys': peer_subkeys[l],
            'peer_u': peer_u[l].astype(BF16), 'peer_v': peer_v[l].astype(BF16),
            'ln2_g': ln2_g[l].reshape(1, -1), 'ln2_b': ln2_b[l].reshape(1, -1),
        }
        w_in_bf = w_in[l].astype(BF16)
        lam_p = da_lambda[l]
        sub_g = da_subln_g[l].reshape(1, -1)

        u_p = matmul(xp_bf, w_in_bf)
        o_a = diff_attn_prompt(u_p, lam_p, sub_g, batch, seq, lam_init)
        o_b = moba_prompt(u_p, batch, seq)
        o_c, tail = conv_prompt(u_p, conv_w[l], batch, seq)
        xp, xp_bf = _token_stage(xp, xp_bf, o_a, o_b, o_c, wts)
        u3 = u_p.reshape(batch, seq, IN_WIDTH)
        st_p[0].append(u3[:, :, OFF_KA:OFF_KA + DA_K].reshape(batch, seq, DA_KV_HEADS, 2 * DA_HEAD_DIM))
        st_p[1].append(u3[:, :, OFF_VA:OFF_VA + DA_V].reshape(batch, seq, DA_KV_HEADS, 2 * DA_HEAD_DIM))
        st_p[2].append(u3[:, :, OFF_KB:OFF_KB + MB_W].reshape(batch, seq, MB_HEADS, MB_HEAD_DIM))
        st_p[3].append(u3[:, :, OFF_VB:OFF_VB + MB_W].reshape(batch, seq, MB_HEADS, MB_HEAD_DIM))
        st_p[4].append(tail[:, 8 - (CONV_W - 1):, :])

        u_s = matmul(xs_bf, w_in_bf)[:n_s].reshape(db, t_len, IN_WIDTH)
        seg = lambda off, width: u_s[:, :, off:off + width]
        pad_rows = lambda a: jnp.pad(a, ((0, 0), (0, 8 - t_len), (0, 0)))
        q_a = seg(OFF_QA, DA_Q).reshape(db, t_len, DA_KV_HEADS, 2, 2, DA_HEAD_DIM)
        q_a = q_a.transpose(0, 3, 1, 2, 4, 5).reshape(db, 2 * t_len, 2 * DA_KV_HEADS, DA_HEAD_DIM)
        qbd_a = _block_diag_queries(q_a, 128)
        k_a, v_a = seg(OFF_KA, DA_K), seg(OFF_VA, DA_V)
        oa_s = diff_attn_sample(qbd_a, ck_a, cv_a, pad_rows(k_a), pad_rows(v_a), page_table, lam_p, sub_g,
                                l, lam_init, t_len)
        oa_s = oa_s.reshape(db, DA_KV_HEADS, 2, t_len, 2 * DA_HEAD_DIM).transpose(0, 3, 1, 2, 4)
        oa_s = oa_s.reshape(n_s, DA_Q)

        q_b = seg(OFF_QB, MB_W).reshape(db, t_len, MB_HEADS, MB_HEAD_DIM)
        qbd_b = _block_diag_queries(q_b, 128)
        k_b, v_b = seg(OFF_KB, MB_W), seg(OFF_VB, MB_W)
        gs = moba_gate_sample(qbd_b, ck_b, page_table, l)
        ob_full = moba_attn_sample(qbd_b, gs, ck_b, cv_b, pad_rows(k_b), pad_rows(v_b), page_table, l, t_len)
        ob_s = ob_full[:, :MB_HEADS * t_len, :].reshape(db, MB_HEADS, t_len, MB_HEADS, MB_HEAD_DIM)
        ob_s = jnp.einsum('bhtkd,hk->bthd', ob_s, jnp.eye(MB_HEADS, dtype=F32)).reshape(n_s, MB_W)

        tmaj = lambda off: seg(off, CONV_DIM).transpose(1, 0, 2)
        oc_s, nst = conv_sample(tmaj(OFF_XC), tmaj(OFF_GB), tmaj(OFF_GC), state_conv[l].transpose(1, 0, 2),
                                conv_w[l])
        oc_s = oc_s.transpose(1, 0, 2).reshape(n_s, CONV_DIM)

        pad_tok = lambda a: jnp.pad(a, ((0, pad_s - n_s), (0, 0))).astype(BF16)
        xs, xs_bf = _token_stage(xs, xs_bf, pad_tok(oa_s), pad_tok(ob_s), pad_tok(oc_s), wts)
        st_s[0].append(k_a.reshape(db, t_len, DA_KV_HEADS, 2 * DA_HEAD_DIM))
        st_s[1].append(v_a.reshape(db, t_len, DA_KV_HEADS, 2 * DA_HEAD_DIM))
        st_s[2].append(k_b.reshape(db, t_len, MB_HEADS, MB_HEAD_DIM))
        st_s[3].append(v_b.reshape(db, t_len, MB_HEADS, MB_HEAD_DIM))
        st_s[4].append(nst.transpose(1, 0, 2))

    y_p = xp.reshape(batch, seq, d)
    y_s = xs[:n_s].reshape(db, t_len, d)
    stack = lambda parts: jnp.stack(parts, axis=0)
    return (y_p, y_s, stack(st_p[0]), stack(st_p[1]), stack(st_p[2]), stack(st_p[3]), stack(st_p[4]),
            stack(st_s[0]), stack(st_s[1]), stack(st_s[2]), stack(st_s[3]), stack(st_s[4]))
```

```python
import functools
import math

import jax
import jax.numpy as jnp
from jax import lax
from jax.experimental import pallas as pl
from jax.experimental.pallas import tpu as pltpu

F32 = jnp.float32
BF16 = jnp.bfloat16

D_MODEL = 4096
DEPTH = 2
PAGE_SIZE = 128
DA_HEADS = 8
DA_KV_HEADS = 4
DA_HEAD_DIM = 128
DA_Q = DA_HEADS * 2 * DA_HEAD_DIM
DA_K = DA_KV_HEADS * 2 * DA_HEAD_DIM
DA_V = DA_K
MB_HEADS = 8
MB_HEAD_DIM = 128
MB_W = MB_HEADS * MB_HEAD_DIM
MB_BLOCK = 256
MB_TOPK = 3
CONV_DIM = 1024
CONV_W = 3
IN_WIDTH = DA_Q + DA_K + DA_V + 3 * MB_W + 3 * CONV_DIM
PEER_HEADS = 8
PEER_N_KEYS = 128
PEER_N_EXPERTS = PEER_N_KEYS * PEER_N_KEYS
PEER_KEY_DIM = 256
PEER_TOPK = 16
ALPHA = (2.0 * DEPTH) ** 0.25
LN_EPS = 1e-5
NEG_INF = -1e30

OFF_QA = 0
OFF_KA = DA_Q
OFF_VA = OFF_KA + DA_K
OFF_QB = OFF_VA + DA_V
OFF_KB = OFF_QB + MB_W
OFF_VB = OFF_KB + MB_W
OFF_XC = OFF_VB + MB_W
OFF_GB = OFF_XC + CONV_DIM
OFF_GC = OFF_GB + CONV_DIM

VMEM_LIMIT = 56 * 1024 * 1024

_NT = (((1,), (1,)), ((), ()))
_TN = (((0,), (0,)), ((), ()))


def _cparams(sem):
    return pltpu.CompilerParams(dimension_semantics=sem, vmem_limit_bytes=VMEM_LIMIT)


def _mm_kernel(x_ref, w_ref, o_ref):
    o_ref[...] = jnp.dot(x_ref[...], w_ref[...], preferred_element_type=F32).astype(o_ref.dtype)


def matmul(x, w, out_dtype=F32, tm=1024, tn=512):
    m, k = x.shape
    n = w.shape[1]
    tm = min(tm, m)
    tn = min(tn, n)
    return pl.pallas_call(
        _mm_kernel,
        grid=(m // tm, n // tn),
        in_specs=[pl.BlockSpec((tm, k), lambda i, j: (i, 0)),
                  pl.BlockSpec((k, tn), lambda i, j: (0, j))],
        out_specs=pl.BlockSpec((tm, tn), lambda i, j: (i, j)),
        out_shape=jax.ShapeDtypeStruct((m, n), out_dtype),
        compiler_params=_cparams(("parallel", "arbitrary")),
        name="matmul",
    )(x, w)


def _merge_kernel(x_ref, oa_ref, ob_ref, oc_ref, wga_ref, wgb_ref, wgc_ref,
                  bga_ref, bgb_ref, bgc_ref, wa_ref, wb_ref, wc_ref, o_ref):
    x = x_ref[...]

    def branch(o_r, wg_r, bg_r, w_r):
        gate = jax.nn.sigmoid(jnp.dot(x, wg_r[...], preferred_element_type=F32) + bg_r[...])
        return gate * jnp.dot(o_r[...], w_r[...], preferred_element_type=F32)

    merged = (branch(oa_ref, wga_ref, bga_ref, wa_ref) + branch(ob_ref, wgb_ref, bgb_ref, wb_ref)
              + branch(oc_ref, wgc_ref, bgc_ref, wc_ref))
    o_ref[...] = merged.astype(o_ref.dtype)


def gated_merge(x, o_a, o_b, o_c, w_gate, b_gate, w_a, w_b, w_c, tm=512, tn=256):
    m, d = x.shape
    tm = min(tm, m)
    nj = d // tn
    row = lambda i, j: (i, 0)
    return pl.pallas_call(
        _merge_kernel,
        grid=(m // tm, nj),
        in_specs=[pl.BlockSpec((tm, d), row),
                  pl.BlockSpec((tm, o_a.shape[1]), row),
                  pl.BlockSpec((tm, o_b.shape[1]), row),
                  pl.BlockSpec((tm, o_c.shape[1]), row),
                  pl.BlockSpec((d, tn), lambda i, j: (0, j)),
                  pl.BlockSpec((d, tn), lambda i, j: (0, j + nj)),
                  pl.BlockSpec((d, tn), lambda i, j: (0, j + 2 * nj)),
                  pl.BlockSpec((1, tn), lambda i, j: (0, j)),
                  pl.BlockSpec((1, tn), lambda i, j: (0, j + nj)),
                  pl.BlockSpec((1, tn), lambda i, j: (0, j + 2 * nj)),
                  pl.BlockSpec((w_a.shape[0], tn), lambda i, j: (0, j)),
                  pl.BlockSpec((w_b.shape[0], tn), lambda i, j: (0, j)),
                  pl.BlockSpec((w_c.shape[0], tn), lambda i, j: (0, j))],
        out_specs=pl.BlockSpec((tm, tn), lambda i, j: (i, j)),
        out_shape=jax.ShapeDtypeStruct((m, d), BF16),
        compiler_params=_cparams(("parallel", "arbitrary")),
        name="gated_merge",
    )(x, o_a, o_b, o_c, w_gate, w_gate, w_gate, b_gate, b_gate, b_gate, w_a, w_b, w_c)


def _layer_norm(z, g, b):
    mu = jnp.mean(z, axis=-1, keepdims=True)
    zc = z - mu
    var = jnp.mean(zc * zc, axis=-1, keepdims=True)
    return zc * lax.rsqrt(var + LN_EPS) * g + b


def _mm_ln_kernel(a_ref, w_ref, res_ref, g_ref, b_ref, o_ref, obf_ref, acc_ref):
    k = pl.program_id(1)

    @pl.when(k == 0)
    def _():
        acc_ref[...] = jnp.zeros_like(acc_ref)

    acc_ref[...] += jnp.dot(a_ref[...], w_ref[...], preferred_element_type=F32)

    @pl.when(k == pl.num_programs(1) - 1)
    def _():
        y = _layer_norm(ALPHA * res_ref[...] + acc_ref[...], g_ref[...], b_ref[...])
        o_ref[...] = y
        obf_ref[...] = y.astype(BF16)


def matmul_residual_ln(a, w, res, g, b, tm=256, tk=512):
    m, kdim = a.shape
    d = w.shape[1]
    tm = min(tm, m)
    return pl.pallas_call(
        _mm_ln_kernel,
        grid=(m // tm, kdim // tk),
        in_specs=[pl.BlockSpec((tm, tk), lambda i, k: (i, k)),
                  pl.BlockSpec((tk, d), lambda i, k: (k, 0)),
                  pl.BlockSpec((tm, d), lambda i, k: (i, 0)),
                  pl.BlockSpec((1, d), lambda i, k: (0, 0)),
                  pl.BlockSpec((1, d), lambda i, k: (0, 0))],
        out_specs=[pl.BlockSpec((tm, d), lambda i, k: (i, 0)),
                   pl.BlockSpec((tm, d), lambda i, k: (i, 0))],
        out_shape=[jax.ShapeDtypeStruct((m, d), F32), jax.ShapeDtypeStruct((m, d), BF16)],
        scratch_shapes=[pltpu.VMEM((tm, d), F32)],
        compiler_params=_cparams(("parallel", "arbitrary")),
        name="matmul_residual_ln",
    )(a, w, res, g, b)


def _add_ln_kernel(res_ref, f_ref, g_ref, b_ref, o_ref, obf_ref):
    y = _layer_norm(ALPHA * res_ref[...] + f_ref[...], g_ref[...], b_ref[...])
    o_ref[...] = y
    obf_ref[...] = y.astype(BF16)


def add_ln(res, f, g, b, tm=256):
    m, d = res.shape
    tm = min(tm, m)
    spec = pl.BlockSpec((tm, d), lambda i: (i, 0))
    vec = pl.BlockSpec((1, d), lambda i: (0, 0))
    return pl.pallas_call(
        _add_ln_kernel,
        grid=(m // tm,),
        in_specs=[spec, spec, vec, vec],
        out_specs=[spec, spec],
        out_shape=[jax.ShapeDtypeStruct((m, d), F32), jax.ShapeDtypeStruct((m, d), BF16)],
        compiler_params=_cparams(("parallel",)),
        name="add_ln",
    )(res, f, g, b)


def _lambda_value(lp, lam_init):
    a = jnp.sum(lp[0:1, :] * lp[1:2, :], axis=-1, keepdims=True)
    b = jnp.sum(lp[2:3, :] * lp[3:4, :], axis=-1, keepdims=True)
    return jnp.exp(a) - jnp.exp(b) + lam_init


def _online_update(s, v_bf, m_ref, l_ref, acc_ref):
    m_prev = m_ref[...]
    m_new = jnp.maximum(m_prev, jnp.max(s, axis=-1, keepdims=True))
    alpha = jnp.exp(m_prev - m_new)
    p = jnp.exp(s - m_new)
    l_ref[...] = alpha * l_ref[...] + jnp.sum(p, axis=-1, keepdims=True)
    acc_ref[...] = alpha * acc_ref[...] + jnp.dot(p.astype(BF16), v_bf, preferred_element_type=F32)
    m_ref[...] = m_new


def _subln(o, g, lam_init):
    return o * lax.rsqrt(jnp.mean(o * o, axis=-1, keepdims=True) + LN_EPS) * g * (1.0 - lam_init)


def _diff_prompt_kernel(q_ref, k_ref, v_ref, lam_ref, g_ref, o_ref, m_ref, l_ref, acc_ref, *, lam_init, tq):
    i = pl.program_id(2)
    dh = DA_HEAD_DIM
    scale = dh ** -0.5
    lam = _lambda_value(lam_ref[...], lam_init)
    row = lax.broadcasted_iota(jnp.int32, (2 * tq, tq), 0)
    col = lax.broadcasted_iota(jnp.int32, (2 * tq, tq), 1)
    causal = col <= jnp.where(row >= tq, row - tq, row)
    qs = [jnp.concatenate([q_ref[:, c * dh:(c + 1) * dh], q_ref[:, (2 + c) * dh:(3 + c) * dh]],
                          axis=0).astype(BF16) for c in range(2)]
    m_ref[...] = jnp.full_like(m_ref, NEG_INF)
    l_ref[...] = jnp.zeros_like(l_ref)
    acc_ref[...] = jnp.zeros_like(acc_ref)

    def tile(j, masked):
        start = pl.multiple_of(j * tq, tq)
        vt = v_ref[pl.ds(start, tq), :].astype(BF16)
        for c in range(2):
            kt = k_ref[pl.ds(start, tq), c * dh:(c + 1) * dh].astype(BF16)
            s = lax.dot_general(qs[c], kt, _NT, preferred_element_type=F32) * scale
            if masked:
                s = jnp.where(causal, s, NEG_INF)
            _online_update(s, vt, m_ref.at[c], l_ref.at[c], acc_ref.at[c])

    def body(j, carry):
        tile(j, False)
        return carry

    lax.fori_loop(0, i, body, 0)
    tile(i, True)
    o = acc_ref[0] / l_ref[0] - lam * (acc_ref[1] / l_ref[1])
    o = _subln(o, g_ref[...], lam_init).astype(o_ref.dtype)
    o_ref[:, 0:2 * dh] = o[:tq]
    o_ref[:, 2 * dh:4 * dh] = o[tq:]


def diff_attn_prompt(u, lam_p, subln_g, batch, seq, lam_init, tq=256):
    nq = seq // tq
    qw = 4 * DA_HEAD_DIM
    kw = 2 * DA_HEAD_DIM
    kern = functools.partial(_diff_prompt_kernel, lam_init=lam_init, tq=tq)
    return pl.pallas_call(
        kern,
        grid=(batch, DA_KV_HEADS, nq),
        in_specs=[pl.BlockSpec((tq, qw), lambda b, k, i: (b * nq + i, k)),
                  pl.BlockSpec((seq, kw), lambda b, k, i: (b, OFF_KA // kw + k)),
                  pl.BlockSpec((seq, kw), lambda b, k, i: (b, OFF_VA // kw + k)),
                  pl.BlockSpec((4, DA_HEAD_DIM), lambda b, k, i: (0, 0)),
                  pl.BlockSpec((1, kw), lambda b, k, i: (0, 0))],
        out_specs=pl.BlockSpec((tq, qw), lambda b, k, i: (b * nq + i, k)),
        out_shape=jax.ShapeDtypeStruct((batch * seq, DA_Q), BF16),
        scratch_shapes=[pltpu.VMEM((2, 2 * tq, 1), F32), pltpu.VMEM((2, 2 * tq, 1), F32),
                        pltpu.VMEM((2, 2 * tq, kw), F32)],
        compiler_params=_cparams(("parallel", "parallel", "arbitrary")),
        name="diff_attn_prompt",
    )(u, u, u, lam_p, subln_g)


def _moba_prompt_kernel(q_ref, k_ref, v_ref, o_ref, km_ref, sel_ref, m_ref, l_ref, acc_ref, *, nb, hg):
    i = pl.program_id(2)
    blk = MB_BLOCK
    dh = MB_HEAD_DIM
    scale = dh ** -0.5

    @pl.when(i == 0)
    def _():
        km_ref[...] = jnp.zeros_like(km_ref)
        for n in range(nb):
            km_ref[n:n + 1, :] = jnp.mean(k_ref[n * blk:(n + 1) * blk, :], axis=0, keepdims=True)

    qs = []
    for h in range(hg):
        qf = q_ref[:, h * dh:(h + 1) * dh]
        qs.append(qf.astype(BF16))
        gs = lax.dot_general(qf, km_ref[:, h * dh:(h + 1) * dh], _NT, preferred_element_type=F32,
                             precision=lax.Precision.HIGHEST)
        lane = lax.broadcasted_iota(jnp.int32, gs.shape, 1)
        past = lane < i
        for n in range(nb):
            gn = gs[:, n:n + 1]
            ahead = (gs > gn) | ((gs == gn) & (lane < n))
            rank = jnp.sum(jnp.where(past & ahead, 1.0, 0.0), axis=-1, keepdims=True)
            sel_ref[h, n] = jnp.where(rank < MB_TOPK, 1.0, 0.0)

    row = lax.broadcasted_iota(jnp.int32, (blk, blk), 0)
    col = lax.broadcasted_iota(jnp.int32, (blk, blk), 1)
    m_ref[...] = jnp.full_like(m_ref, NEG_INF)
    l_ref[...] = jnp.zeros_like(l_ref)
    acc_ref[...] = jnp.zeros_like(acc_ref)

    def tile(n, mask_of):
        start = pl.multiple_of(n * blk, blk)
        for h in range(hg):
            kt = k_ref[pl.ds(start, blk), h * dh:(h + 1) * dh].astype(BF16)
            vt = v_ref[pl.ds(start, blk), h * dh:(h + 1) * dh].astype(BF16)
            s = lax.dot_general(qs[h], kt, _NT, preferred_element_type=F32) * scale
            s = jnp.where(mask_of(h), s, NEG_INF)
            _online_update(s, vt, m_ref.at[h], l_ref.at[h], acc_ref.at[h])

    tile(i, lambda h: col <= row)

    def body(n, carry):
        tile(n, lambda h: sel_ref[h, n] > 0.5)
        return carry

    lax.fori_loop(0, i, body, 0)
    for h in range(hg):
        o_ref[:, h * dh:(h + 1) * dh] = (acc_ref[h] / l_ref[h]).astype(o_ref.dtype)


def moba_prompt(u, batch, seq, hg=4):
    nb = seq // MB_BLOCK
    dh = MB_HEAD_DIM
    w = hg * dh
    kern = functools.partial(_moba_prompt_kernel, nb=nb, hg=hg)
    return pl.pallas_call(
        kern,
        grid=(batch, MB_HEADS // hg, nb),
        in_specs=[pl.BlockSpec((MB_BLOCK, w), lambda b, h, i: (b * nb + i, OFF_QB // w + h)),
                  pl.BlockSpec((seq, w), lambda b, h, i: (b, OFF_KB // w + h)),
                  pl.BlockSpec((seq, w), lambda b, h, i: (b, OFF_VB // w + h))],
        out_specs=pl.BlockSpec((MB_BLOCK, w), lambda b, h, i: (b * nb + i, h)),
        out_shape=jax.ShapeDtypeStruct((batch * seq, MB_W), BF16),
        scratch_shapes=[pltpu.VMEM((128, w), F32), pltpu.VMEM((hg, nb, MB_BLOCK, 1), F32),
                        pltpu.VMEM((hg, MB_BLOCK, 1), F32), pltpu.VMEM((hg, MB_BLOCK, 1), F32),
                        pltpu.VMEM((hg, MB_BLOCK, dh), F32)],
        compiler_params=_cparams(("parallel", "parallel", "arbitrary")),
        name="moba_prompt",
    )(u, u, u)


def _conv_prompt_kernel(xc_ref, gb_ref, gc_ref, w_ref, o_ref, st_ref):
    z = gc_ref[...] * xc_ref[...]
    t = z.shape[0]
    row = lax.broadcasted_iota(jnp.int32, z.shape, 0)
    z1 = jnp.where(row >= 1, pltpu.roll(z, 1, 0), 0.0)
    z2 = jnp.where(row >= 2, pltpu.roll(z, 2, 0), 0.0)
    y = w_ref[0:1, :] * z2 + w_ref[1:2, :] * z1 + w_ref[2:3, :] * z
    o_ref[...] = (gb_ref[...] * y).astype(o_ref.dtype)
    st_ref[0] = z[t - 8:, :]


def conv_prompt(u, conv_w, batch, seq, tc=256):
    nc = CONV_DIM // tc
    blk = lambda off: pl.BlockSpec((seq, tc), lambda b, c: (b, off // tc + c))
    return pl.pallas_call(
        _conv_prompt_kernel,
        grid=(batch, nc),
        in_specs=[blk(OFF_XC), blk(OFF_GB), blk(OFF_GC), pl.BlockSpec((CONV_W, tc), lambda b, c: (0, c))],
        out_specs=[pl.BlockSpec((seq, tc), lambda b, c: (b, c)),
                   pl.BlockSpec((1, 8, tc), lambda b, c: (b, 0, c))],
        out_shape=[jax.ShapeDtypeStruct((batch * seq, CONV_DIM), BF16),
                   jax.ShapeDtypeStruct((batch, 8, CONV_DIM), F32)],
        compiler_params=_cparams(("parallel", "parallel")),
        name="conv_prompt",
    )(u, u, u, conv_w)


def _conv_sample_kernel(xc_ref, gb_ref, gc_ref, st_ref, w_ref, o_ref, nst_ref):
    t_len = xc_ref.shape[0]
    zz = [st_ref[0], st_ref[1]] + [gc_ref[t] * xc_ref[t] for t in range(t_len)]
    for t in range(t_len):
        y = w_ref[0:1, :] * zz[t] + w_ref[1:2, :] * zz[t + 1] + w_ref[2:3, :] * zz[t + 2]
        o_ref[t] = gb_ref[t] * y
    nst_ref[0] = zz[t_len]
    nst_ref[1] = zz[t_len + 1]


def conv_sample(xc, gb, gc, state, conv_w):
    return pl.pallas_call(
        _conv_sample_kernel,
        out_shape=[jax.ShapeDtypeStruct(xc.shape, F32), jax.ShapeDtypeStruct(state.shape, F32)],
        name="conv_sample",
    )(xc, gb, gc, state, conv_w)


def _page_specs(layer, pg, heads, dh):
    return [pl.BlockSpec((None, None, PAGE_SIZE, heads, dh),
                         lambda b, p, pt, j=j: (layer, pt[b, p * pg + j], 0, 0, 0)) for j in range(pg)]


def _token_head_rows(page_refs, dtype=BF16):
    ps, heads, dh = page_refs[0].shape
    return jnp.concatenate([r[...].reshape(ps * heads, dh).astype(dtype) for r in page_refs], axis=0)


def split_component_view(x):
    lead = x.shape[:-2]
    kv, w = x.shape[-2:]
    n = len(lead)
    x = x.reshape(*lead, kv, 2, w // 2)
    return x.transpose(*range(n), n + 1, n, n + 2).reshape(*lead, 2 * kv, w // 2)


def _swap_row_halves(x):
    n, d = x.shape
    return pltpu.roll(x.reshape(n // 8, 8, d), 4, 1).reshape(n, d)


def _diff_sample_kernel(pt_ref, q_ref, *refs, lam_init, t_len, pg):
    k_refs, v_refs = refs[:pg], refs[pg:2 * pg]
    kn_ref, vn_ref, lam_ref, g_ref, o_ref, m_ref, l_ref, acc_ref = refs[2 * pg:]
    p = pl.program_id(1)
    dh = DA_HEAD_DIM
    kv = DA_KV_HEADS
    scale = dh ** -0.5
    per_comp = 2 * t_len
    rows = kv * 2 * per_comp
    q = q_ref[0].astype(BF16)

    def own_rows(n_keys):
        r = lax.broadcasted_iota(jnp.int32, (rows, n_keys), 0)
        j = lax.broadcasted_iota(jnp.int32, (rows, n_keys), 1)
        target = ((r // per_comp) % 2) * kv + r // (2 * per_comp)
        return (j % (2 * kv)) == target, r, j

    def values(v_f32):
        return jnp.concatenate([v_f32, _swap_row_halves(v_f32)], axis=-1).astype(BF16)

    @pl.when(p == 0)
    def _():
        m_ref[...] = jnp.full_like(m_ref, NEG_INF)
        l_ref[...] = jnp.zeros_like(l_ref)
        acc_ref[...] = jnp.zeros_like(acc_ref)

    s = lax.dot_general(q, _token_head_rows(k_refs), _NT, preferred_element_type=F32) * scale
    own, _, _ = own_rows(s.shape[1])
    s = jnp.where(own, s, NEG_INF)
    _online_update(s, values(_token_head_rows(v_refs, F32)), m_ref, l_ref, acc_ref)

    @pl.when(p == pl.num_programs(1) - 1)
    def _():
        sn = lax.dot_general(q, kn_ref[0].astype(BF16), _NT, preferred_element_type=F32) * scale
        own_n, r, j = own_rows(sn.shape[1])
        sn = jnp.where(own_n & ((j // (2 * kv)) <= (r % t_len)), sn, NEG_INF)
        _online_update(sn, values(vn_ref[0]), m_ref, l_ref, acc_ref)
        o = acc_ref[...] / l_ref[...]
        lam = _lambda_value(lam_ref[...], lam_init)
        for k in range(kv):
            base = k * 2 * per_comp
            o1 = o[base:base + per_comp, :]
            o2 = o[base + per_comp:base + 2 * per_comp, :]
            o2 = jnp.concatenate([o2[:, dh:], o2[:, :dh]], axis=-1)
            o_ref[0, k] = _subln(o1 - lam * o2, g_ref[...], lam_init)


def diff_attn_sample(q, cache_k, cache_v, k_new, v_new, page_table, lam_p, subln_g, layer, lam_init, t_len,
                     pg=8):
    db, n_pages = page_table.shape
    rows, dh = q.shape[1:]
    per_comp = 2 * t_len
    kern = functools.partial(_diff_sample_kernel, lam_init=lam_init, t_len=t_len, pg=pg)
    pages = _page_specs(layer, pg, 2 * DA_KV_HEADS, dh)
    per_b = lambda shape: pl.BlockSpec(shape, lambda b, p, pt: (b, 0, 0))
    n_new = k_new.shape[1]
    grid_spec = pltpu.PrefetchScalarGridSpec(
        num_scalar_prefetch=1,
        grid=(db, n_pages // pg),
        in_specs=[per_b((1, rows, dh))] + pages + pages + [
            per_b((1, n_new, dh)), per_b((1, n_new, dh)),
            pl.BlockSpec((4, dh), lambda b, p, pt: (0, 0)),
            pl.BlockSpec((1, 2 * dh), lambda b, p, pt: (0, 0))],
        out_specs=pl.BlockSpec((1, DA_KV_HEADS, per_comp, 2 * dh), lambda b, p, pt: (b, 0, 0, 0)),
        scratch_shapes=[pltpu.VMEM((rows, 1), F32), pltpu.VMEM((rows, 1), F32), pltpu.VMEM((rows, 2 * dh), F32)],
    )
    return pl.pallas_call(
        kern,
        grid_spec=grid_spec,
        out_shape=jax.ShapeDtypeStruct((db, DA_KV_HEADS, per_comp, 2 * dh), F32),
        compiler_params=_cparams(("parallel", "arbitrary")),
        name="diff_attn_sample",
    )(page_table, q, *([cache_k] * pg), *([cache_v] * pg), k_new, v_new, lam_p, subln_g)


def _moba_gate_kernel(pt_ref, q_ref, *refs, pg, t_len):
    k_refs, gs_ref = refs[:pg], refs[pg]
    p = pl.program_id(1)
    pages_per_block = MB_BLOCK // PAGE_SIZE
    blocks = pg // pages_per_block

    @pl.when(p == 0)
    def _():
        gs_ref[...] = jnp.zeros_like(gs_ref)

    lane = lax.broadcasted_iota(jnp.int32, gs_ref.shape[2:], 1)
    for blk in range(blocks):
        page_sums = [jnp.sum(k_refs[blk * pages_per_block + j][...], axis=0) for j in range(pages_per_block)]
        km = functools.reduce(lambda a, b: a + b, page_sums) / MB_BLOCK
        n = p * blocks + blk
        for t in range(t_len):
            g = jnp.sum(q_ref[0, t] * km, axis=-1, keepdims=True)
            gs_ref[0, t] += jnp.where(lane == n, g, 0.0)


def moba_gate_sample(q, cache_k, page_table, layer, pg=8):
    db, n_pages = page_table.shape
    t_len = q.shape[1]
    kern = functools.partial(_moba_gate_kernel, pg=pg, t_len=t_len)
    grid_spec = pltpu.PrefetchScalarGridSpec(
        num_scalar_prefetch=1,
        grid=(db, n_pages // pg),
        in_specs=[pl.BlockSpec((1, t_len, MB_HEADS, MB_HEAD_DIM), lambda b, p, pt: (b, 0, 0, 0))]
        + _page_specs(layer, pg, MB_HEADS, MB_HEAD_DIM),
        out_specs=pl.BlockSpec((1, t_len, MB_HEADS, 128), lambda b, p, pt: (b, 0, 0, 0)),
    )
    return pl.pallas_call(
        kern,
        grid_spec=grid_spec,
        out_shape=jax.ShapeDtypeStruct((db, t_len, MB_HEADS, 128), F32),
        compiler_params=_cparams(("parallel", "arbitrary")),
        name="moba_gate_sample",
    )(page_table, q, *([cache_k] * pg))


def _moba_sample_kernel(pt_ref, q_ref, gs_ref, *refs, t_len, n_blocks, pg):
    k_refs, v_refs = refs[:pg], refs[pg:2 * pg]
    kn_ref, vn_ref, o_ref, sel_ref, m_ref, l_ref, acc_ref = refs[2 * pg:]
    p = pl.program_id(1)
    scale = MB_HEAD_DIM ** -0.5
    heads = MB_HEADS
    rows = heads * t_len
    block_keys = MB_BLOCK * heads
    blocks = pg * PAGE_SIZE // MB_BLOCK
    q = q_ref[0].astype(BF16)
    lane = lax.broadcasted_iota(jnp.int32, (rows, 128), 1).astype(F32)

    def head_match(n_keys):
        r = lax.broadcasted_iota(jnp.int32, (rows, n_keys), 0)
        j = lax.broadcasted_iota(jnp.int32, (rows, n_keys), 1)
        return (j % heads) == (r // t_len), r, j

    @pl.when(p == 0)
    def _():
        cur = jnp.where(lane < n_blocks, gs_ref[0], -jnp.inf)
        sel = jnp.zeros((rows, 128), F32)
        for _ in range(MB_TOPK):
            mx = jnp.max(cur, axis=-1, keepdims=True)
            first = jnp.min(jnp.where(cur == mx, lane, 128.0), axis=-1, keepdims=True)
            hit = lane == first
            sel = jnp.where(hit, 1.0, sel)
            cur = jnp.where(hit, -jnp.inf, cur)
        sel_ref[...] = sel
        m_ref[...] = jnp.full_like(m_ref, NEG_INF)
        l_ref[...] = jnp.zeros_like(l_ref)
        acc_ref[...] = jnp.zeros_like(acc_ref)
        sn = lax.dot_general(q, kn_ref[0].astype(BF16), _NT, preferred_element_type=F32) * scale
        same_head, r, j = head_match(sn.shape[1])
        sn = jnp.where(same_head & ((j // heads) <= (r % t_len)), sn, NEG_INF)
        _online_update(sn, vn_ref[0].astype(BF16), m_ref, l_ref, acc_ref)

    sel = sel_ref[...]
    chosen = []
    for blk in range(blocks):
        block = (p * blocks + blk).astype(F32)
        hit = jnp.sum(jnp.where(lane == block, sel, 0.0), axis=-1, keepdims=True)
        chosen.append(jnp.broadcast_to(hit, (rows, block_keys)))
    s = lax.dot_general(q, _token_head_rows(k_refs), _NT, preferred_element_type=F32) * scale
    same_head, _, _ = head_match(s.shape[1])
    s = jnp.where(same_head & (jnp.concatenate(chosen, axis=-1) > 0.5), s, NEG_INF)
    _online_update(s, _token_head_rows(v_refs), m_ref, l_ref, acc_ref)

    @pl.when(p == pl.num_programs(1) - 1)
    def _():
        o_ref[0] = acc_ref[...] / l_ref[...]


def moba_attn_sample(q, gs, cache_k, cache_v, k_new, v_new, page_table, layer, pg=8):
    db, n_pages = page_table.shape
    rows, dh = q.shape[1:]
    t_len = rows // MB_HEADS
    n_blocks = n_pages * PAGE_SIZE // MB_BLOCK
    kern = functools.partial(_moba_sample_kernel, t_len=t_len, n_blocks=n_blocks, pg=pg)
    pages = _page_specs(layer, pg, MB_HEADS, MB_HEAD_DIM)
    per_b = lambda shape: pl.BlockSpec(shape, lambda b, p, pt: (b, 0, 0))
    grid_spec = pltpu.PrefetchScalarGridSpec(
        num_scalar_prefetch=1,
        grid=(db, n_pages // pg),
        in_specs=[per_b((1, rows, dh)), per_b((1, rows, 128))] + pages + pages + [
            per_b((1, rows, dh)), per_b((1, rows, dh))],
        out_specs=per_b((1, rows, dh)),
        scratch_shapes=[pltpu.VMEM((rows, 128), F32), pltpu.VMEM((rows, 1), F32), pltpu.VMEM((rows, 1), F32),
                        pltpu.VMEM((rows, dh), F32)],
    )
    return pl.pallas_call(
        kern,
        grid_spec=grid_spec,
        out_shape=jax.ShapeDtypeStruct((db, rows, dh), F32),
        compiler_params=_cparams(("parallel", "arbitrary")),
        name="moba_attn_sample",
    )(page_table, q, gs, *([cache_k] * pg), *([cache_v] * pg), k_new, v_new)


def _extract_topk(cur_ref, out_ref, k):
    groups = cur_ref.shape[0]

    def body(j, carry):
        for g in range(groups):
            cur = cur_ref[g]
            mx = jnp.max(cur, axis=0, keepdims=True)
            out_ref[g, pl.ds(j, 1), :] = mx
            cur_ref[g] = jnp.where(cur == mx, -jnp.inf, cur)
        return carry

    lax.fori_loop(0, k, body, 0)


_CAND_ROWS = PEER_TOPK + 7 * 8 + 8


def _peer_route_kernel(q_ref, sk_ref, s1m_ref, s2m_ref, eb_ref, a0_ref, thr_ref,
                       cur_ref, top_ref, cand_ref, best_ref):
    half = PEER_KEY_DIM // 2
    k = PEER_TOPK
    for h in range(PEER_HEADS):
        for c, raw_ref in enumerate((s1m_ref, s2m_ref)):
            qc = q_ref[:, (h * 2 + c) * half:(h * 2 + c + 1) * half]
            s = lax.dot_general(sk_ref[h, c], qc, _NT, preferred_element_type=F32,
                                precision=lax.Precision.HIGHEST)
            raw_ref[h] = s
            cur_ref[h * 2 + c] = s
    _extract_topk(cur_ref, top_ref, k)
    for h in range(PEER_HEADS):
        ta = top_ref[2 * h]
        tb = top_ref[2 * h + 1]
        cand_ref[h, 0:k, :] = ta[0:1, :] + tb
        for p in range(1, 8):
            cand_ref[h, k + (p - 1) * 8:k + p * 8, :] = ta[p:p + 1, :] + tb[0:8, :]
        cand_ref[h, k + 56:k + 64, :] = ta[8:16, :] + tb[0:1, :]
    _extract_topk(cand_ref, best_ref, k)
    for h in range(PEER_HEADS):
        ta = top_ref[2 * h]
        tb = top_ref[2 * h + 1]
        best = best_ref[h]
        z = jnp.sum(jnp.exp(best - best[0:1, :]), axis=0, keepdims=True)
        s1 = s1m_ref[h]
        s2 = s2m_ref[h]
        s1m_ref[h] = jnp.where(s1 >= ta[k - 1:k, :], s1, NEG_INF)
        s2m_ref[h] = jnp.where(s2 >= tb[k - 1:k, :], s2, NEG_INF)
        eb_ref[h] = jnp.exp(s2 - tb[0:1, :]) / z
        a0_ref[h:h + 1, :] = ta[0:1, :]
        thr_ref[h:h + 1, :] = best[k - 1:k, :]


def peer_route(q, sub_keys, tm=256):
    m = q.shape[0]
    tm = min(tm, m)
    nk = PEER_N_KEYS
    big = pl.BlockSpec((PEER_HEADS, nk, tm), lambda i: (0, 0, i))
    small = pl.BlockSpec((PEER_HEADS, tm), lambda i: (0, i))
    big_shape = jax.ShapeDtypeStruct((PEER_HEADS, nk, m), F32)
    small_shape = jax.ShapeDtypeStruct((PEER_HEADS, m), F32)
    return pl.pallas_call(
        _peer_route_kernel,
        grid=(m // tm,),
        in_specs=[pl.BlockSpec((tm, PEER_HEADS * PEER_KEY_DIM), lambda i: (i, 0)),
                  pl.BlockSpec(sub_keys.shape, lambda i: (0, 0, 0, 0))],
        out_specs=[big, big, big, small, small],
        out_shape=[big_shape, big_shape, big_shape, small_shape, small_shape],
        scratch_shapes=[pltpu.VMEM((2 * PEER_HEADS, nk, tm), F32),
                        pltpu.VMEM((2 * PEER_HEADS, PEER_TOPK, tm), F32),
                        pltpu.VMEM((PEER_HEADS, _CAND_ROWS, tm), F32),
                        pltpu.VMEM((PEER_HEADS, PEER_TOPK, tm), F32)],
        compiler_params=_cparams(("parallel",)),
        name="peer_route",
    )(q, sub_keys)


def _peer_expert_kernel(x_ref, u_ref, v_ref, s1m_ref, s2m_ref, eb_ref, a0_ref, thr_ref, o_ref, a_ref, *, te):
    e = pl.program_id(1)
    nk = PEER_N_KEYS

    @pl.when(e == 0)
    def _():
        o_ref[...] = jnp.zeros_like(o_ref)

    g = lax.dot_general(u_ref[...], x_ref[...], _NT, preferred_element_type=F32)
    for ii in range(te // nk):
        i = e * (te // nk) + ii
        w = jnp.zeros((nk, g.shape[1]), F32)
        for h in range(PEER_HEADS):
            s1 = s1m_ref[h, pl.ds(i, 1), :]
            ea = jnp.exp(s1 - a0_ref[h:h + 1, :])
            hit = (s1 + s2m_ref[h]) >= thr_ref[h:h + 1, :]
            w = w + jnp.where(hit, ea * eb_ref[h], 0.0)
        gi = g[ii * nk:(ii + 1) * nk, :]
        act = 0.5 * gi * (1.0 + lax.erf(gi * (2.0 ** -0.5)))
        a_ref[ii * nk:(ii + 1) * nk, :] = (w * act).astype(BF16)
    o_ref[...] += lax.dot_general(a_ref[...], v_ref[...], _TN, preferred_element_type=F32)


def peer_experts(x, u, v, s1m, s2m, eb, a0, thr, tm=512, te=512):
    m, d = x.shape
    tm = min(tm, m)
    n_e = u.shape[0]
    kern = functools.partial(_peer_expert_kernel, te=te)
    once = pl.Buffered(1)
    big = pl.BlockSpec((PEER_HEADS, PEER_N_KEYS, tm), lambda i, e: (0, 0, i), pipeline_mode=once)
    small = pl.BlockSpec((PEER_HEADS, tm), lambda i, e: (0, i), pipeline_mode=once)
    return pl.pallas_call(
        kern,
        grid=(m // tm, n_e // te),
        in_specs=[pl.BlockSpec((tm, d), lambda i, e: (i, 0), pipeline_mode=once),
                  pl.BlockSpec((te, d), lambda i, e: (e, 0)),
                  pl.BlockSpec((te, d), lambda i, e: (e, 0)),
                  big, big, big, small, small],
        out_specs=pl.BlockSpec((tm, d), lambda i, e: (i, 0)),
        out_shape=jax.ShapeDtypeStruct((m, d), F32),
        scratch_shapes=[pltpu.VMEM((te, tm), BF16)],
        compiler_params=_cparams(("parallel", "arbitrary")),
        name="peer_experts",
    )(x, u, v, s1m, s2m, eb, a0, thr)


def _token_stage(x, x_bf, o_a, o_b, o_c, wts):
    merged = gated_merge(x_bf, o_a, o_b, o_c, wts['w_gate'], wts['b_gate'], wts['w_br_a'], wts['w_br_b'],
                         wts['w_br_c'])
    h, h_bf = matmul_residual_ln(merged, wts['w_o'], x, wts['ln1_g'], wts['ln1_b'])
    q = matmul(h_bf, wts['peer_wq'])
    s1m, s2m, eb, a0, thr = peer_route(q, wts['peer_subkeys'])
    f = peer_experts(h_bf, wts['peer_u'], wts['peer_v'], s1m, s2m, eb, a0, thr)
    return add_ln(h, f, wts['ln2_g'], wts['ln2_b'])


def kernel(x_prompt, x_sample, cache_diff_k, cache_diff_v, cache_moba_k, cache_moba_v, state_conv, page_table,
           w_in, da_lambda, da_subln_g, conv_w, w_gate, b_gate, w_br_a, w_br_b, w_br_c, w_o, ln1_g, ln1_b,
           peer_wq, peer_subkeys, peer_u, peer_v, ln2_g, ln2_b):
    batch, seq, d = x_prompt.shape
    db, t_len, _ = x_sample.shape
    n_p = batch * seq
    n_s = db * t_len
    pad_s = 128
    ck_a, cv_a = split_component_view(cache_diff_k), split_component_view(cache_diff_v)
    ck_b, cv_b = cache_moba_k, cache_moba_v

    xp = x_prompt.reshape(n_p, d)
    xs = jnp.pad(x_sample.reshape(n_s, d), ((0, pad_s - n_s), (0, 0)))
    xp_bf = xp.astype(BF16)
    xs_bf = xs.astype(BF16)

    st_p = [[] for _ in range(5)]
    st_s = [[] for _ in range(5)]
    for l in range(DEPTH):
        lam_init = 0.8 - 0.6 * math.exp(-0.3 * l)
        wts = {
            'w_gate': w_gate[l].astype(BF16), 'b_gate': b_gate[l].reshape(1, -1),
            'w_br_a': w_br_a[l].astype(BF16), 'w_br_b': w_br_b[l].astype(BF16), 'w_br_c': w_br_c[l].astype(BF16),
            'w_o': w_o[l].astype(BF16), 'ln1_g': ln1_g[l].reshape(1, -1), 'ln1_b': ln1_b[l].reshape(1, -1),
            'peer_wq': peer_wq[l].astype(BF16), 'peer_subkeys': peer_subkeys[l],
            'peer_u': peer_u[l].astype(BF16), 'peer_v': peer_v[l].astype(BF16),
            'ln2_g': ln2_g[l].reshape(1, -1), 'ln2_b': ln2_b[l].reshape(1, -1),
        }
        w_in_bf = w_in[l].astype(BF16)
        lam_p = da_lambda[l]
        sub_g = da_subln_g[l].reshape(1, -1)

        u_p = matmul(xp_bf, w_in_bf)
        o_a = diff_attn_prompt(u_p, lam_p, sub_g, batch, seq, lam_init)
        o_b = moba_prompt(u_p, batch, seq)
        o_c, tail = conv_prompt(u_p, conv_w[l], batch, seq)
        xp, xp_bf = _token_stage(xp, xp_bf, o_a, o_b, o_c, wts)
        u3 = u_p.reshape(batch, seq, IN_WIDTH)
        st_p[0].append(u3[:, :, OFF_KA:OFF_KA + DA_K].reshape(batch, seq, DA_KV_HEADS, 2 * DA_HEAD_DIM))
        st_p[1].append(u3[:, :, OFF_VA:OFF_VA + DA_V].reshape(batch, seq, DA_KV_HEADS, 2 * DA_HEAD_DIM))
        st_p[2].append(u3[:, :, OFF_KB:OFF_KB + MB_W].reshape(batch, seq, MB_HEADS, MB_HEAD_DIM))
        st_p[3].append(u3[:, :, OFF_VB:OFF_VB + MB_W].reshape(batch, seq, MB_HEADS, MB_HEAD_DIM))
        st_p[4].append(tail[:, 8 - (CONV_W - 1):, :])

        u_s = matmul(xs_bf, w_in_bf)[:n_s].reshape(db, t_len, IN_WIDTH)
        seg = lambda off, width: u_s[:, :, off:off + width]
        q_a = seg(OFF_QA, DA_Q).reshape(db, t_len, DA_KV_HEADS, 2, 2, DA_HEAD_DIM)
        q_a = q_a.transpose(0, 2, 4, 3, 1, 5).reshape(db, 4 * DA_KV_HEADS * t_len, DA_HEAD_DIM)
        k_a, v_a = seg(OFF_KA, DA_K), seg(OFF_VA, DA_V)
        new_rows = lambda a: split_component_view(a.reshape(db, t_len, DA_KV_HEADS, 2 * DA_HEAD_DIM)).reshape(
            db, t_len * 2 * DA_KV_HEADS, DA_HEAD_DIM)
        oa_s = diff_attn_sample(q_a, ck_a, cv_a, new_rows(k_a), new_rows(v_a), page_table, lam_p, sub_g,
                                l, lam_init, t_len)
        oa_s = oa_s.reshape(db, DA_KV_HEADS, 2, t_len, 2 * DA_HEAD_DIM).transpose(0, 3, 1, 2, 4)
        oa_s = oa_s.reshape(n_s, DA_Q)

        q_b = seg(OFF_QB, MB_W).reshape(db, t_len, MB_HEADS, MB_HEAD_DIM)
        k_b, v_b = seg(OFF_KB, MB_W), seg(OFF_VB, MB_W)
        head_major = lambda a: a.transpose(0, 2, 1, 3).reshape(db, MB_HEADS * t_len, a.shape[-1])
        token_head = lambda a: a.reshape(db, t_len * MB_HEADS, MB_HEAD_DIM)
        gs = moba_gate_sample(q_b, ck_b, page_table, l)
        ob_s = moba_attn_sample(head_major(q_b), head_major(gs), ck_b, cv_b, token_head(k_b), token_head(v_b),
                                page_table, l)
        ob_s = ob_s.reshape(db, MB_HEADS, t_len, MB_HEAD_DIM).transpose(0, 2, 1, 3).reshape(n_s, MB_W)

        tmaj = lambda off: seg(off, CONV_DIM).transpose(1, 0, 2)
        oc_s, nst = conv_sample(tmaj(OFF_XC), tmaj(OFF_GB), tmaj(OFF_GC), state_conv[l].transpose(1, 0, 2),
                                conv_w[l])
        oc_s = oc_s.transpose(1, 0, 2).reshape(n_s, CONV_DIM)

        pad_tok = lambda a: jnp.pad(a, ((0, pad_s - n_s), (0, 0))).astype(BF16)
        xs, xs_bf = _token_stage(xs, xs_bf, pad_tok(oa_s), pad_tok(ob_s), pad_tok(oc_s), wts)
        st_s[0].append(k_a.reshape(db, t_len, DA_KV_HEADS, 2 * DA_HEAD_DIM))
        st_s[1].append(v_a.reshape(db, t_len, DA_KV_HEADS, 2 * DA_HEAD_DIM))
        st_s[2].append(k_b.reshape(db, t_len, MB_HEADS, MB_HEAD_DIM))
        st_s[3].append(v_b.reshape(db, t_len, MB_HEADS, MB_HEAD_DIM))
        st_s[4].append(nst.transpose(1, 0, 2))

    y_p = xp.reshape(batch, seq, d)
    y_s = xs[:n_s].reshape(db, t_len, d)
    stack = lambda parts: jnp.stack(parts, axis=0)
    return (y_p, y_s, stack(st_p[0]), stack(st_p[1]), stack(st_p[2]), stack(st_p[3]), stack(st_p[4]),
            stack(st_s[0]), stack(st_s[1]), stack(st_s[2]), stack(st_s[3]), stack(st_s[4]))
```

```python
import functools
import math

import jax
import jax.numpy as jnp
from jax import lax
from jax.experimental import pallas as pl
from jax.experimental.pallas import tpu as pltpu

F32 = jnp.float32
BF16 = jnp.bfloat16

D_MODEL = 4096
DEPTH = 2
PAGE_SIZE = 128
DA_HEADS = 8
DA_KV_HEADS = 4
DA_HEAD_DIM = 128
DA_Q = DA_HEADS * 2 * DA_HEAD_DIM
DA_K = DA_KV_HEADS * 2 * DA_HEAD_DIM
DA_V = DA_K
MB_HEADS = 8
MB_HEAD_DIM = 128
MB_W = MB_HEADS * MB_HEAD_DIM
MB_BLOCK = 256
MB_TOPK = 3
CONV_DIM = 1024
CONV_W = 3
IN_WIDTH = DA_Q + DA_K + DA_V + 3 * MB_W + 3 * CONV_DIM
PEER_HEADS = 8
PEER_N_KEYS = 128
PEER_N_EXPERTS = PEER_N_KEYS * PEER_N_KEYS
PEER_KEY_DIM = 256
PEER_TOPK = 16
ALPHA = (2.0 * DEPTH) ** 0.25
LN_EPS = 1e-5
NEG_INF = -1e30

OFF_QA = 0
OFF_KA = DA_Q
OFF_VA = OFF_KA + DA_K
OFF_QB = OFF_VA + DA_V
OFF_KB = OFF_QB + MB_W
OFF_VB = OFF_KB + MB_W
OFF_XC = OFF_VB + MB_W
OFF_GB = OFF_XC + CONV_DIM
OFF_GC = OFF_GB + CONV_DIM

VMEM_LIMIT = 56 * 1024 * 1024

_NT = (((1,), (1,)), ((), ()))
_TN = (((0,), (0,)), ((), ()))


def _cparams(sem):
    return pltpu.CompilerParams(dimension_semantics=sem, vmem_limit_bytes=VMEM_LIMIT)


def _mm_kernel(x_ref, w_ref, o_ref):
    o_ref[...] = jnp.dot(x_ref[...], w_ref[...], preferred_element_type=F32).astype(o_ref.dtype)


def matmul(x, w, out_dtype=F32, tm=1024, tn=512):
    m, k = x.shape
    n = w.shape[1]
    tm = min(tm, m)
    tn = min(tn, n)
    return pl.pallas_call(
        _mm_kernel,
        grid=(m // tm, n // tn),
        in_specs=[pl.BlockSpec((tm, k), lambda i, j: (i, 0)),
                  pl.BlockSpec((k, tn), lambda i, j: (0, j))],
        out_specs=pl.BlockSpec((tm, tn), lambda i, j: (i, j)),
        out_shape=jax.ShapeDtypeStruct((m, n), out_dtype),
        compiler_params=_cparams(("parallel", "arbitrary")),
        name="matmul",
    )(x, w)


def _merge_kernel(x_ref, oa_ref, ob_ref, oc_ref, wga_ref, wgb_ref, wgc_ref,
                  bga_ref, bgb_ref, bgc_ref, wa_ref, wb_ref, wc_ref, o_ref):
    x = x_ref[...]

    def branch(o_r, wg_r, bg_r, w_r):
        gate = jax.nn.sigmoid(jnp.dot(x, wg_r[...], preferred_element_type=F32) + bg_r[...])
        return gate * jnp.dot(o_r[...], w_r[...], preferred_element_type=F32)

    merged = (branch(oa_ref, wga_ref, bga_ref, wa_ref) + branch(ob_ref, wgb_ref, bgb_ref, wb_ref)
              + branch(oc_ref, wgc_ref, bgc_ref, wc_ref))
    o_ref[...] = merged.astype(o_ref.dtype)


def gated_merge(x, o_a, o_b, o_c, w_gate, b_gate, w_a, w_b, w_c, tm=512, tn=256):
    m, d = x.shape
    tm = min(tm, m)
    nj = d // tn
    row = lambda i, j: (i, 0)
    return pl.pallas_call(
        _merge_kernel,
        grid=(m // tm, nj),
        in_specs=[pl.BlockSpec((tm, d), row),
                  pl.BlockSpec((tm, o_a.shape[1]), row),
                  pl.BlockSpec((tm, o_b.shape[1]), row),
                  pl.BlockSpec((tm, o_c.shape[1]), row),
                  pl.BlockSpec((d, tn), lambda i, j: (0, j)),
                  pl.BlockSpec((d, tn), lambda i, j: (0, j + nj)),
                  pl.BlockSpec((d, tn), lambda i, j: (0, j + 2 * nj)),
                  pl.BlockSpec((1, tn), lambda i, j: (0, j)),
                  pl.BlockSpec((1, tn), lambda i, j: (0, j + nj)),
                  pl.BlockSpec((1, tn), lambda i, j: (0, j + 2 * nj)),
                  pl.BlockSpec((w_a.shape[0], tn), lambda i, j: (0, j)),
                  pl.BlockSpec((w_b.shape[0], tn), lambda i, j: (0, j)),
                  pl.BlockSpec((w_c.shape[0], tn), lambda i, j: (0, j))],
        out_specs=pl.BlockSpec((tm, tn), lambda i, j: (i, j)),
        out_shape=jax.ShapeDtypeStruct((m, d), BF16),
        compiler_params=_cparams(("parallel", "arbitrary")),
        name="gated_merge",
    )(x, o_a, o_b, o_c, w_gate, w_gate, w_gate, b_gate, b_gate, b_gate, w_a, w_b, w_c)


def _layer_norm(z, g, b):
    mu = jnp.mean(z, axis=-1, keepdims=True)
    zc = z - mu
    var = jnp.mean(zc * zc, axis=-1, keepdims=True)
    return zc * lax.rsqrt(var + LN_EPS) * g + b


def _mm_ln_kernel(a_ref, w_ref, res_ref, g_ref, b_ref, o_ref, obf_ref):
    k = pl.program_id(1)

    @pl.when(k == 0)
    def _():
        o_ref[...] = jnp.zeros_like(o_ref)

    o_ref[...] += jnp.dot(a_ref[...], w_ref[...], preferred_element_type=F32)

    @pl.when(k == pl.num_programs(1) - 1)
    def _():
        y = _layer_norm(ALPHA * res_ref[...] + o_ref[...], g_ref[...], b_ref[...])
        o_ref[...] = y
        obf_ref[...] = y.astype(BF16)


def matmul_residual_ln(a, w, res, g, b, tm=512, tk=512):
    m, kdim = a.shape
    d = w.shape[1]
    tm = min(tm, m)
    return pl.pallas_call(
        _mm_ln_kernel,
        grid=(m // tm, kdim // tk),
        in_specs=[pl.BlockSpec((tm, tk), lambda i, k: (i, k)),
                  pl.BlockSpec((tk, d), lambda i, k: (k, 0)),
                  pl.BlockSpec((tm, d), lambda i, k: (i, 0), pipeline_mode=pl.Buffered(1)),
                  pl.BlockSpec((1, d), lambda i, k: (0, 0)),
                  pl.BlockSpec((1, d), lambda i, k: (0, 0))],
        out_specs=[pl.BlockSpec((tm, d), lambda i, k: (i, 0)),
                   pl.BlockSpec((tm, d), lambda i, k: (i, 0))],
        out_shape=[jax.ShapeDtypeStruct((m, d), F32), jax.ShapeDtypeStruct((m, d), BF16)],
        compiler_params=_cparams(("parallel", "arbitrary")),
        name="matmul_residual_ln",
    )(a, w, res, g, b)


def _add_ln_kernel(res_ref, f_ref, g_ref, b_ref, o_ref, obf_ref):
    y = _layer_norm(ALPHA * res_ref[...] + f_ref[...], g_ref[...], b_ref[...])
    o_ref[...] = y
    obf_ref[...] = y.astype(BF16)


def add_ln(res, f, g, b, tm=256):
    m, d = res.shape
    tm = min(tm, m)
    spec = pl.BlockSpec((tm, d), lambda i: (i, 0))
    vec = pl.BlockSpec((1, d), lambda i: (0, 0))
    return pl.pallas_call(
        _add_ln_kernel,
        grid=(m // tm,),
        in_specs=[spec, spec, vec, vec],
        out_specs=[spec, spec],
        out_shape=[jax.ShapeDtypeStruct((m, d), F32), jax.ShapeDtypeStruct((m, d), BF16)],
        compiler_params=_cparams(("parallel",)),
        name="add_ln",
    )(res, f, g, b)


def _lambda_value(lp, lam_init):
    a = jnp.sum(lp[0:1, :] * lp[1:2, :], axis=-1, keepdims=True)
    b = jnp.sum(lp[2:3, :] * lp[3:4, :], axis=-1, keepdims=True)
    return jnp.exp(a) - jnp.exp(b) + lam_init


def _online_update(s, v_bf, m_ref, l_ref, acc_ref):
    m_prev = m_ref[...]
    m_new = jnp.maximum(m_prev, jnp.max(s, axis=-1, keepdims=True))
    alpha = jnp.exp(m_prev - m_new)
    p = jnp.exp(s - m_new)
    l_ref[...] = alpha * l_ref[...] + jnp.sum(p, axis=-1, keepdims=True)
    acc_ref[...] = alpha * acc_ref[...] + jnp.dot(p.astype(BF16), v_bf, preferred_element_type=F32)
    m_ref[...] = m_new


def _subln(o, g, lam_init):
    return o * lax.rsqrt(jnp.mean(o * o, axis=-1, keepdims=True) + LN_EPS) * g * (1.0 - lam_init)


def _online_update_keymajor(s, vt_bf, m_ref, l_ref, acc_ref):
    m_prev = m_ref[...]
    m_new = jnp.maximum(m_prev, jnp.max(s, axis=0, keepdims=True))
    alpha = jnp.exp(m_prev - m_new)
    p = jnp.exp(s - m_new)
    l_ref[...] = alpha * l_ref[...] + jnp.sum(p, axis=0, keepdims=True)
    acc_ref[...] = alpha * acc_ref[...] + jnp.dot(vt_bf, p.astype(BF16), preferred_element_type=F32)
    m_ref[...] = m_new


def _diff_prompt_kernel(q_ref, k_ref, v_ref, lam_ref, g_ref, o_ref, vt_ref, m_ref, l_ref, acc_ref, *,
                        lam_init, tq):
    i = pl.program_id(2)
    dh = DA_HEAD_DIM
    scale = dh ** -0.5
    n_kv = v_ref.shape[0] // tq

    @pl.when(i == 0)
    def _():
        for j in range(n_kv):
            vt_ref[j] = v_ref[j * tq:(j + 1) * tq, :].T.astype(BF16)

    lam = _lambda_value(lam_ref[...], lam_init)
    key = lax.broadcasted_iota(jnp.int32, (tq, 2 * tq), 0)
    qry = lax.broadcasted_iota(jnp.int32, (tq, 2 * tq), 1)
    causal = key <= jnp.where(qry >= tq, qry - tq, qry)
    qs = [(jnp.concatenate([q_ref[:, c * dh:(c + 1) * dh], q_ref[:, (2 + c) * dh:(3 + c) * dh]], axis=0)
           * scale).astype(BF16) for c in range(2)]
    m_ref[...] = jnp.full_like(m_ref, NEG_INF)
    l_ref[...] = jnp.zeros_like(l_ref)
    acc_ref[...] = jnp.zeros_like(acc_ref)

    def tile(j, masked):
        start = pl.multiple_of(j * tq, tq)
        vt = vt_ref[j]
        for c in range(2):
            kt = k_ref[pl.ds(start, tq), c * dh:(c + 1) * dh].astype(BF16)
            s = lax.dot_general(kt, qs[c], _NT, preferred_element_type=F32)
            if masked:
                s = jnp.where(causal, s, NEG_INF)
            _online_update_keymajor(s, vt, m_ref.at[c], l_ref.at[c], acc_ref.at[c])

    def body(j, carry):
        tile(j, False)
        return carry

    lax.fori_loop(0, i, body, 0)
    tile(i, True)
    ot = acc_ref[0] / l_ref[0] - lam * (acc_ref[1] / l_ref[1])
    ot = ot * lax.rsqrt(jnp.mean(ot * ot, axis=0, keepdims=True) + LN_EPS) * g_ref[...] * (1.0 - lam_init)
    o = ot.T.astype(o_ref.dtype)
    o_ref[:, 0:2 * dh] = o[:tq]
    o_ref[:, 2 * dh:4 * dh] = o[tq:]


def diff_attn_prompt(u, lam_p, subln_g, batch, seq, lam_init, tq=256):
    nq = seq // tq
    qw = 4 * DA_HEAD_DIM
    kw = 2 * DA_HEAD_DIM
    kern = functools.partial(_diff_prompt_kernel, lam_init=lam_init, tq=tq)
    return pl.pallas_call(
        kern,
        grid=(batch, DA_KV_HEADS, nq),
        in_specs=[pl.BlockSpec((tq, qw), lambda b, k, i: (b * nq + i, k)),
                  pl.BlockSpec((seq, kw), lambda b, k, i: (b, OFF_KA // kw + k)),
                  pl.BlockSpec((seq, kw), lambda b, k, i: (b, OFF_VA // kw + k)),
                  pl.BlockSpec((4, DA_HEAD_DIM), lambda b, k, i: (0, 0)),
                  pl.BlockSpec((kw, 1), lambda b, k, i: (0, 0))],
        out_specs=pl.BlockSpec((tq, qw), lambda b, k, i: (b * nq + i, k)),
        out_shape=jax.ShapeDtypeStruct((batch * seq, DA_Q), BF16),
        scratch_shapes=[pltpu.VMEM((nq, kw, tq), BF16),
                        pltpu.VMEM((2, 1, 2 * tq), F32), pltpu.VMEM((2, 1, 2 * tq), F32),
                        pltpu.VMEM((2, kw, 2 * tq), F32)],
        compiler_params=_cparams(("parallel", "parallel", "arbitrary")),
        name="diff_attn_prompt",
    )(u, u, u, lam_p, subln_g.reshape(kw, 1))


def _moba_prompt_kernel(q_ref, k_ref, v_ref, o_ref, km_ref, vt_ref, sel_ref, m_ref, l_ref, acc_ref, *, nb, hg):
    i = pl.program_id(2)
    blk = MB_BLOCK
    dh = MB_HEAD_DIM
    scale = dh ** -0.5

    @pl.when(i == 0)
    def _():
        km_ref[...] = jnp.zeros_like(km_ref)
        for n in range(nb):
            km_ref[n:n + 1, :] = jnp.mean(k_ref[n * blk:(n + 1) * blk, :], axis=0, keepdims=True)
            for h in range(hg):
                vt_ref[h, n] = v_ref[n * blk:(n + 1) * blk, h * dh:(h + 1) * dh].T.astype(BF16)

    qs = []
    for h in range(hg):
        qf = q_ref[:, h * dh:(h + 1) * dh]
        qs.append((qf * scale).astype(BF16))
        gs = lax.dot_general(km_ref[:, h * dh:(h + 1) * dh], qf, _NT, preferred_element_type=F32,
                             precision=lax.Precision.HIGHEST)
        block = lax.broadcasted_iota(jnp.int32, gs.shape, 0)
        past = block < i
        for n in range(nb):
            gn = gs[n:n + 1, :]
            ahead = (gs > gn) | ((gs == gn) & (block < n))
            rank = jnp.sum(jnp.where(past & ahead, 1.0, 0.0), axis=0, keepdims=True)
            sel_ref[h, n] = jnp.where(rank < MB_TOPK, 1.0, 0.0)

    key = lax.broadcasted_iota(jnp.int32, (blk, blk), 0)
    qry = lax.broadcasted_iota(jnp.int32, (blk, blk), 1)
    m_ref[...] = jnp.full_like(m_ref, NEG_INF)
    l_ref[...] = jnp.zeros_like(l_ref)
    acc_ref[...] = jnp.zeros_like(acc_ref)

    def tile(n, mask_of):
        start = pl.multiple_of(n * blk, blk)
        for h in range(hg):
            kt = k_ref[pl.ds(start, blk), h * dh:(h + 1) * dh].astype(BF16)
            s = lax.dot_general(kt, qs[h], _NT, preferred_element_type=F32)
            s = jnp.where(mask_of(h), s, NEG_INF)
            _online_update_keymajor(s, vt_ref[h, n], m_ref.at[h], l_ref.at[h], acc_ref.at[h])

    tile(i, lambda h: key <= qry)

    def body(n, carry):
        tile(n, lambda h: sel_ref[h, n] > 0.5)
        return carry

    lax.fori_loop(0, i, body, 0)
    for h in range(hg):
        o_ref[:, h * dh:(h + 1) * dh] = (acc_ref[h] / l_ref[h]).T.astype(o_ref.dtype)


def moba_prompt(u, batch, seq, hg=4):
    nb = seq // MB_BLOCK
    dh = MB_HEAD_DIM
    w = hg * dh
    kern = functools.partial(_moba_prompt_kernel, nb=nb, hg=hg)
    return pl.pallas_call(
        kern,
        grid=(batch, MB_HEADS // hg, nb),
        in_specs=[pl.BlockSpec((MB_BLOCK, w), lambda b, h, i: (b * nb + i, OFF_QB // w + h)),
                  pl.BlockSpec((seq, w), lambda b, h, i: (b, OFF_KB // w + h)),
                  pl.BlockSpec((seq, w), lambda b, h, i: (b, OFF_VB // w + h))],
        out_specs=pl.BlockSpec((MB_BLOCK, w), lambda b, h, i: (b * nb + i, h)),
        out_shape=jax.ShapeDtypeStruct((batch * seq, MB_W), BF16),
        scratch_shapes=[pltpu.VMEM((-(-nb // 8) * 8, w), F32), pltpu.VMEM((hg, nb, dh, MB_BLOCK), BF16),
                        pltpu.VMEM((hg, nb, 1, MB_BLOCK), F32),
                        pltpu.VMEM((hg, 1, MB_BLOCK), F32), pltpu.VMEM((hg, 1, MB_BLOCK), F32),
                        pltpu.VMEM((hg, dh, MB_BLOCK), F32)],
        compiler_params=_cparams(("parallel", "parallel", "arbitrary")),
        name="moba_prompt",
    )(u, u, u)


def _conv_prompt_kernel(xc_ref, gb_ref, gc_ref, w_ref, o_ref, st_ref):
    z = gc_ref[...] * xc_ref[...]
    t = z.shape[0]
    row = lax.broadcasted_iota(jnp.int32, z.shape, 0)
    z1 = jnp.where(row >= 1, pltpu.roll(z, 1, 0), 0.0)
    z2 = jnp.where(row >= 2, pltpu.roll(z, 2, 0), 0.0)
    y = w_ref[0:1, :] * z2 + w_ref[1:2, :] * z1 + w_ref[2:3, :] * z
    o_ref[...] = (gb_ref[...] * y).astype(o_ref.dtype)
    st_ref[0] = z[t - 8:, :]


def conv_prompt(u, conv_w, batch, seq, tc=256):
    nc = CONV_DIM // tc
    blk = lambda off: pl.BlockSpec((seq, tc), lambda b, c: (b, off // tc + c))
    return pl.pallas_call(
        _conv_prompt_kernel,
        grid=(batch, nc),
        in_specs=[blk(OFF_XC), blk(OFF_GB), blk(OFF_GC), pl.BlockSpec((CONV_W, tc), lambda b, c: (0, c))],
        out_specs=[pl.BlockSpec((seq, tc), lambda b, c: (b, c)),
                   pl.BlockSpec((1, 8, tc), lambda b, c: (b, 0, c))],
        out_shape=[jax.ShapeDtypeStruct((batch * seq, CONV_DIM), BF16),
                   jax.ShapeDtypeStruct((batch, 8, CONV_DIM), F32)],
        compiler_params=_cparams(("parallel", "parallel")),
        name="conv_prompt",
    )(u, u, u, conv_w)


def _conv_sample_kernel(xc_ref, gb_ref, gc_ref, st_ref, w_ref, o_ref, nst_ref):
    t_len = xc_ref.shape[0]
    zz = [st_ref[0], st_ref[1]] + [gc_ref[t] * xc_ref[t] for t in range(t_len)]
    for t in range(t_len):
        y = w_ref[0:1, :] * zz[t] + w_ref[1:2, :] * zz[t + 1] + w_ref[2:3, :] * zz[t + 2]
        o_ref[t] = gb_ref[t] * y
    nst_ref[0] = zz[t_len]
    nst_ref[1] = zz[t_len + 1]


def conv_sample(xc, gb, gc, state, conv_w):
    return pl.pallas_call(
        _conv_sample_kernel,
        out_shape=[jax.ShapeDtypeStruct(xc.shape, F32), jax.ShapeDtypeStruct(state.shape, F32)],
        name="conv_sample",
    )(xc, gb, gc, state, conv_w)


def _page_specs(layer, pg, heads, dh):
    return [pl.BlockSpec((None, None, PAGE_SIZE, heads, dh),
                         lambda b, p, pt, j=j: (layer, pt[b, p * pg + j], 0, 0, 0)) for j in range(pg)]


def _token_head_rows(page_refs, dtype=BF16):
    ps, heads, dh = page_refs[0].shape
    return jnp.concatenate([r[...].reshape(ps * heads, dh).astype(dtype) for r in page_refs], axis=0)


def split_component_view(x):
    lead = x.shape[:-2]
    kv, w = x.shape[-2:]
    n = len(lead)
    x = x.reshape(*lead, kv, 2, w // 2)
    return x.transpose(*range(n), n + 1, n, n + 2).reshape(*lead, 2 * kv, w // 2)


def _swap_row_halves(x):
    n, d = x.shape
    return pltpu.roll(x.reshape(n // 8, 8, d), 4, 1).reshape(n, d)


def _diff_sample_kernel(pt_ref, q_ref, *refs, lam_init, t_len, pg):
    k_refs, v_refs = refs[:pg], refs[pg:2 * pg]
    kn_ref, vn_ref, lam_ref, g_ref, o_ref, m_ref, l_ref, acc_ref = refs[2 * pg:]
    p = pl.program_id(1)
    dh = DA_HEAD_DIM
    kv = DA_KV_HEADS
    scale = dh ** -0.5
    per_comp = 2 * t_len
    rows = kv * 2 * per_comp
    q = q_ref[0].astype(BF16)

    def own_rows(n_keys):
        r = lax.broadcasted_iota(jnp.int32, (rows, n_keys), 0)
        j = lax.broadcasted_iota(jnp.int32, (rows, n_keys), 1)
        target = ((r // per_comp) % 2) * kv + r // (2 * per_comp)
        return (j % (2 * kv)) == target, r, j

    def values(v_f32):
        return jnp.concatenate([v_f32, _swap_row_halves(v_f32)], axis=-1).astype(BF16)

    @pl.when(p == 0)
    def _():
        m_ref[...] = jnp.full_like(m_ref, NEG_INF)
        l_ref[...] = jnp.zeros_like(l_ref)
        acc_ref[...] = jnp.zeros_like(acc_ref)

    s = lax.dot_general(q, _token_head_rows(k_refs), _NT, preferred_element_type=F32) * scale
    own, _, _ = own_rows(s.shape[1])
    s = jnp.where(own, s, NEG_INF)
    _online_update(s, values(_token_head_rows(v_refs, F32)), m_ref, l_ref, acc_ref)

    @pl.when(p == pl.num_programs(1) - 1)
    def _():
        sn = lax.dot_general(q, kn_ref[0].astype(BF16), _NT, preferred_element_type=F32) * scale
        own_n, r, j = own_rows(sn.shape[1])
        sn = jnp.where(own_n & ((j // (2 * kv)) <= (r % t_len)), sn, NEG_INF)
        _online_update(sn, values(vn_ref[0]), m_ref, l_ref, acc_ref)
        o = acc_ref[...] / l_ref[...]
        lam = _lambda_value(lam_ref[...], lam_init)
        for k in range(kv):
            base = k * 2 * per_comp
            o1 = o[base:base + per_comp, :]
            o2 = o[base + per_comp:base + 2 * per_comp, :]
            o2 = jnp.concatenate([o2[:, dh:], o2[:, :dh]], axis=-1)
            o_ref[0, k] = _subln(o1 - lam * o2, g_ref[...], lam_init)


def diff_attn_sample(q, cache_k, cache_v, k_new, v_new, page_table, lam_p, subln_g, layer, lam_init, t_len,
                     pg=8):
    db, n_pages = page_table.shape
    rows, dh = q.shape[1:]
    per_comp = 2 * t_len
    kern = functools.partial(_diff_sample_kernel, lam_init=lam_init, t_len=t_len, pg=pg)
    pages = _page_specs(layer, pg, 2 * DA_KV_HEADS, dh)
    per_b = lambda shape: pl.BlockSpec(shape, lambda b, p, pt: (b, 0, 0))
    n_new = k_new.shape[1]
    grid_spec = pltpu.PrefetchScalarGridSpec(
        num_scalar_prefetch=1,
        grid=(db, n_pages // pg),
        in_specs=[per_b((1, rows, dh))] + pages + pages + [
            per_b((1, n_new, dh)), per_b((1, n_new, dh)),
            pl.BlockSpec((4, dh), lambda b, p, pt: (0, 0)),
            pl.BlockSpec((1, 2 * dh), lambda b, p, pt: (0, 0))],
        out_specs=pl.BlockSpec((1, DA_KV_HEADS, per_comp, 2 * dh), lambda b, p, pt: (b, 0, 0, 0)),
        scratch_shapes=[pltpu.VMEM((rows, 1), F32), pltpu.VMEM((rows, 1), F32), pltpu.VMEM((rows, 2 * dh), F32)],
    )
    return pl.pallas_call(
        kern,
        grid_spec=grid_spec,
        out_shape=jax.ShapeDtypeStruct((db, DA_KV_HEADS, per_comp, 2 * dh), F32),
        compiler_params=_cparams(("parallel", "arbitrary")),
        name="diff_attn_sample",
    )(page_table, q, *([cache_k] * pg), *([cache_v] * pg), k_new, v_new, lam_p, subln_g)


def _moba_gate_kernel(pt_ref, q_ref, *refs, pg, t_len):
    k_refs, gs_ref = refs[:pg], refs[pg]
    p = pl.program_id(1)
    pages_per_block = MB_BLOCK // PAGE_SIZE
    blocks = pg // pages_per_block

    @pl.when(p == 0)
    def _():
        gs_ref[...] = jnp.zeros_like(gs_ref)

    lane = lax.broadcasted_iota(jnp.int32, gs_ref.shape[2:], 1)
    for blk in range(blocks):
        page_sums = [jnp.sum(k_refs[blk * pages_per_block + j][...], axis=0) for j in range(pages_per_block)]
        km = functools.reduce(lambda a, b: a + b, page_sums) / MB_BLOCK
        n = p * blocks + blk
        for t in range(t_len):
            g = jnp.sum(q_ref[0, t] * km, axis=-1, keepdims=True)
            gs_ref[0, t] += jnp.where(lane == n, g, 0.0)


def moba_gate_sample(q, cache_k, page_table, layer, pg=8):
    db, n_pages = page_table.shape
    t_len = q.shape[1]
    kern = functools.partial(_moba_gate_kernel, pg=pg, t_len=t_len)
    grid_spec = pltpu.PrefetchScalarGridSpec(
        num_scalar_prefetch=1,
        grid=(db, n_pages // pg),
        in_specs=[pl.BlockSpec((1, t_len, MB_HEADS, MB_HEAD_DIM), lambda b, p, pt: (b, 0, 0, 0))]
        + _page_specs(layer, pg, MB_HEADS, MB_HEAD_DIM),
        out_specs=pl.BlockSpec((1, t_len, MB_HEADS, 128), lambda b, p, pt: (b, 0, 0, 0)),
    )
    return pl.pallas_call(
        kern,
        grid_spec=grid_spec,
        out_shape=jax.ShapeDtypeStruct((db, t_len, MB_HEADS, 128), F32),
        compiler_params=_cparams(("parallel", "arbitrary")),
        name="moba_gate_sample",
    )(page_table, q, *([cache_k] * pg))


def _moba_sample_kernel(pt_ref, q_ref, gs_ref, *refs, t_len, n_blocks, pg):
    k_refs, v_refs = refs[:pg], refs[pg:2 * pg]
    kn_ref, vn_ref, o_ref, sel_ref, m_ref, l_ref, acc_ref = refs[2 * pg:]
    p = pl.program_id(1)
    scale = MB_HEAD_DIM ** -0.5
    heads = MB_HEADS
    rows = heads * t_len
    block_keys = MB_BLOCK * heads
    blocks = pg * PAGE_SIZE // MB_BLOCK
    q = q_ref[0].astype(BF16)
    lane = lax.broadcasted_iota(jnp.int32, (rows, 128), 1).astype(F32)

    def head_match(n_keys):
        r = lax.broadcasted_iota(jnp.int32, (rows, n_keys), 0)
        j = lax.broadcasted_iota(jnp.int32, (rows, n_keys), 1)
        return (j % heads) == (r // t_len), r, j

    @pl.when(p == 0)
    def _():
        cur = jnp.where(lane < n_blocks, gs_ref[0], -jnp.inf)
        sel = jnp.zeros((rows, 128), F32)
        for _ in range(MB_TOPK):
            mx = jnp.max(cur, axis=-1, keepdims=True)
            first = jnp.min(jnp.where(cur == mx, lane, 128.0), axis=-1, keepdims=True)
            hit = lane == first
            sel = jnp.where(hit, 1.0, sel)
            cur = jnp.where(hit, -jnp.inf, cur)
        sel_ref[...] = sel
        m_ref[...] = jnp.full_like(m_ref, NEG_INF)
        l_ref[...] = jnp.zeros_like(l_ref)
        acc_ref[...] = jnp.zeros_like(acc_ref)
        sn = lax.dot_general(q, kn_ref[0].astype(BF16), _NT, preferred_element_type=F32) * scale
        same_head, r, j = head_match(sn.shape[1])
        sn = jnp.where(same_head & ((j // heads) <= (r % t_len)), sn, NEG_INF)
        _online_update(sn, vn_ref[0].astype(BF16), m_ref, l_ref, acc_ref)

    sel = sel_ref[...]
    chosen = []
    for blk in range(blocks):
        block = (p * blocks + blk).astype(F32)
        hit = jnp.sum(jnp.where(lane == block, sel, 0.0), axis=-1, keepdims=True)
        chosen.append(jnp.broadcast_to(hit, (rows, block_keys)))
    s = lax.dot_general(q, _token_head_rows(k_refs), _NT, preferred_element_type=F32) * scale
    same_head, _, _ = head_match(s.shape[1])
    s = jnp.where(same_head & (jnp.concatenate(chosen, axis=-1) > 0.5), s, NEG_INF)
    _online_update(s, _token_head_rows(v_refs), m_ref, l_ref, acc_ref)

    @pl.when(p == pl.num_programs(1) - 1)
    def _():
        o_ref[0] = acc_ref[...] / l_ref[...]


def moba_attn_sample(q, gs, cache_k, cache_v, k_new, v_new, page_table, layer, pg=8):
    db, n_pages = page_table.shape
    rows, dh = q.shape[1:]
    t_len = rows // MB_HEADS
    n_blocks = n_pages * PAGE_SIZE // MB_BLOCK
    kern = functools.partial(_moba_sample_kernel, t_len=t_len, n_blocks=n_blocks, pg=pg)
    pages = _page_specs(layer, pg, MB_HEADS, MB_HEAD_DIM)
    per_b = lambda shape: pl.BlockSpec(shape, lambda b, p, pt: (b, 0, 0))
    grid_spec = pltpu.PrefetchScalarGridSpec(
        num_scalar_prefetch=1,
        grid=(db, n_pages // pg),
        in_specs=[per_b((1, rows, dh)), per_b((1, rows, 128))] + pages + pages + [
            per_b((1, rows, dh)), per_b((1, rows, dh))],
        out_specs=per_b((1, rows, dh)),
        scratch_shapes=[pltpu.VMEM((rows, 128), F32), pltpu.VMEM((rows, 1), F32), pltpu.VMEM((rows, 1), F32),
                        pltpu.VMEM((rows, dh), F32)],
    )
    return pl.pallas_call(
        kern,
        grid_spec=grid_spec,
        out_shape=jax.ShapeDtypeStruct((db, rows, dh), F32),
        compiler_params=_cparams(("parallel", "arbitrary")),
        name="moba_attn_sample",
    )(page_table, q, gs, *([cache_k] * pg), *([cache_v] * pg), k_new, v_new)


def _extract_topk(cur_ref, out_ref, k):
    groups = cur_ref.shape[0]

    def body(j, carry):
        for g in range(groups):
            cur = cur_ref[g]
            mx = jnp.max(cur, axis=0, keepdims=True)
            out_ref[g, pl.ds(j, 1), :] = mx
            cur_ref[g] = jnp.where(cur == mx, -jnp.inf, cur)
        return carry

    lax.fori_loop(0, k, body, 0)


_CAND_ROWS = PEER_TOPK + 7 * 8 + 8


def _peer_route_kernel(q_ref, sk_ref, s1m_ref, s2m_ref, eb_ref, a0_ref, thr_ref,
                       cur_ref, top_ref, cand_ref, best_ref):
    half = PEER_KEY_DIM // 2
    k = PEER_TOPK
    for h in range(PEER_HEADS):
        for c, raw_ref in enumerate((s1m_ref, s2m_ref)):
            qc = q_ref[:, (h * 2 + c) * half:(h * 2 + c + 1) * half]
            s = lax.dot_general(sk_ref[h, c], qc, _NT, preferred_element_type=F32,
                                precision=lax.Precision.HIGHEST)
            raw_ref[h] = s
            cur_ref[h * 2 + c] = s
    _extract_topk(cur_ref, top_ref, k)
    for h in range(PEER_HEADS):
        ta = top_ref[2 * h]
        tb = top_ref[2 * h + 1]
        cand_ref[h, 0:k, :] = ta[0:1, :] + tb
        for p in range(1, 8):
            cand_ref[h, k + (p - 1) * 8:k + p * 8, :] = ta[p:p + 1, :] + tb[0:8, :]
        cand_ref[h, k + 56:k + 64, :] = ta[8:16, :] + tb[0:1, :]
    _extract_topk(cand_ref, best_ref, k)
    for h in range(PEER_HEADS):
        ta = top_ref[2 * h]
        tb = top_ref[2 * h + 1]
        best = best_ref[h]
        z = jnp.sum(jnp.exp(best - best[0:1, :]), axis=0, keepdims=True)
        s1 = s1m_ref[h]
        s2 = s2m_ref[h]
        s1m_ref[h] = jnp.where(s1 >= ta[k - 1:k, :], s1, NEG_INF)
        s2m_ref[h] = jnp.where(s2 >= tb[k - 1:k, :], s2, NEG_INF)
        eb_ref[h] = jnp.exp(s2 - tb[0:1, :]) / z
        a0_ref[h:h + 1, :] = ta[0:1, :]
        thr_ref[h:h + 1, :] = best[k - 1:k, :]


def peer_route(q, sub_keys, tm=256):
    m = q.shape[0]
    tm = min(tm, m)
    nk = PEER_N_KEYS
    big = pl.BlockSpec((PEER_HEADS, nk, tm), lambda i: (0, 0, i))
    small = pl.BlockSpec((PEER_HEADS, tm), lambda i: (0, i))
    big_shape = jax.ShapeDtypeStruct((PEER_HEADS, nk, m), F32)
    small_shape = jax.ShapeDtypeStruct((PEER_HEADS, m), F32)
    return pl.pallas_call(
        _peer_route_kernel,
        grid=(m // tm,),
        in_specs=[pl.BlockSpec((tm, PEER_HEADS * PEER_KEY_DIM), lambda i: (i, 0)),
                  pl.BlockSpec(sub_keys.shape, lambda i: (0, 0, 0, 0))],
        out_specs=[big, big, big, small, small],
        out_shape=[big_shape, big_shape, big_shape, small_shape, small_shape],
        scratch_shapes=[pltpu.VMEM((2 * PEER_HEADS, nk, tm), F32),
                        pltpu.VMEM((2 * PEER_HEADS, PEER_TOPK, tm), F32),
                        pltpu.VMEM((PEER_HEADS, _CAND_ROWS, tm), F32),
                        pltpu.VMEM((PEER_HEADS, PEER_TOPK, tm), F32)],
        compiler_params=_cparams(("parallel",)),
        name="peer_route",
    )(q, sub_keys)


def _peer_expert_kernel(x_ref, u_ref, v_ref, s1m_ref, s2m_ref, eb_ref, a0_ref, thr_ref, o_ref, a_ref, *, te):
    e = pl.program_id(1)
    nk = PEER_N_KEYS

    @pl.when(e == 0)
    def _():
        o_ref[...] = jnp.zeros_like(o_ref)

    g = lax.dot_general(u_ref[...], x_ref[...], _NT, preferred_element_type=F32)
    for ii in range(te // nk):
        i = e * (te // nk) + ii
        w = jnp.zeros((nk, g.shape[1]), F32)
        for h in range(PEER_HEADS):
            s1 = s1m_ref[h, pl.ds(i, 1), :]
            ea = jnp.exp(s1 - a0_ref[h:h + 1, :])
            hit = (s1 + s2m_ref[h]) >= thr_ref[h:h + 1, :]
            w = w + jnp.where(hit, ea * eb_ref[h], 0.0)
        gi = g[ii * nk:(ii + 1) * nk, :]
        act = 0.5 * gi * (1.0 + lax.erf(gi * (2.0 ** -0.5)))
        a_ref[ii * nk:(ii + 1) * nk, :] = (w * act).astype(BF16)
    o_ref[...] += lax.dot_general(a_ref[...], v_ref[...], _TN, preferred_element_type=F32)


def peer_experts(x, u, v, s1m, s2m, eb, a0, thr, tm=512, te=512):
    m, d = x.shape
    tm = min(tm, m)
    n_e = u.shape[0]
    kern = functools.partial(_peer_expert_kernel, te=te)
    once = pl.Buffered(1)
    big = pl.BlockSpec((PEER_HEADS, PEER_N_KEYS, tm), lambda i, e: (0, 0, i), pipeline_mode=once)
    small = pl.BlockSpec((PEER_HEADS, tm), lambda i, e: (0, i), pipeline_mode=once)
    return pl.pallas_call(
        kern,
        grid=(m // tm, n_e // te),
        in_specs=[pl.BlockSpec((tm, d), lambda i, e: (i, 0), pipeline_mode=once),
                  pl.BlockSpec((te, d), lambda i, e: (e, 0)),
                  pl.BlockSpec((te, d), lambda i, e: (e, 0)),
                  big, big, big, small, small],
        out_specs=pl.BlockSpec((tm, d), lambda i, e: (i, 0)),
        out_shape=jax.ShapeDtypeStruct((m, d), F32),
        scratch_shapes=[pltpu.VMEM((te, tm), BF16)],
        compiler_params=_cparams(("parallel", "arbitrary")),
        name="peer_experts",
    )(x, u, v, s1m, s2m, eb, a0, thr)


def _token_stage(x, x_bf, o_a, o_b, o_c, wts):
    merged = gated_merge(x_bf, o_a, o_b, o_c, wts['w_gate'], wts['b_gate'], wts['w_br_a'], wts['w_br_b'],
                         wts['w_br_c'])
    h, h_bf = matmul_residual_ln(merged, wts['w_o'], x, wts['ln1_g'], wts['ln1_b'])
    q = matmul(h_bf, wts['peer_wq'])
    s1m, s2m, eb, a0, thr = peer_route(q, wts['peer_subkeys'])
    f = peer_experts(h_bf, wts['peer_u'], wts['peer_v'], s1m, s2m, eb, a0, thr)
    return add_ln(h, f, wts['ln2_g'], wts['ln2_b'])


def kernel(x_prompt, x_sample, cache_diff_k, cache_diff_v, cache_moba_k, cache_moba_v, state_conv, page_table,
           w_in, da_lambda, da_subln_g, conv_w, w_gate, b_gate, w_br_a, w_br_b, w_br_c, w_o, ln1_g, ln1_b,
           peer_wq, peer_subkeys, peer_u, peer_v, ln2_g, ln2_b):
    batch, seq, d = x_prompt.shape
    db, t_len, _ = x_sample.shape
    n_p = batch * seq
    n_s = db * t_len
    pad_s = 128
    ck_a, cv_a = split_component_view(cache_diff_k), split_component_view(cache_diff_v)
    ck_b, cv_b = cache_moba_k, cache_moba_v

    xp = x_prompt.reshape(n_p, d)
    xs = jnp.pad(x_sample.reshape(n_s, d), ((0, pad_s - n_s), (0, 0)))
    xp_bf = xp.astype(BF16)
    xs_bf = xs.astype(BF16)

    st_p = [[] for _ in range(5)]
    st_s = [[] for _ in range(5)]
    for l in range(DEPTH):
        lam_init = 0.8 - 0.6 * math.exp(-0.3 * l)
        wts = {
            'w_gate': w_gate[l].astype(BF16), 'b_gate': b_gate[l].reshape(1, -1),
            'w_br_a': w_br_a[l].astype(BF16), 'w_br_b': w_br_b[l].astype(BF16), 'w_br_c': w_br_c[l].astype(BF16),
            'w_o': w_o[l].astype(BF16), 'ln1_g': ln1_g[l].reshape(1, -1), 'ln1_b': ln1_b[l].reshape(1, -1),
            'peer_wq': peer_wq[l].astype(BF16), 'peer_subkeys': peer_subkeys[l],
            'peer_u': peer_u[l].astype(BF16), 'peer_v': peer_v[l].astype(BF16),
            'ln2_g': ln2_g[l].reshape(1, -1), 'ln2_b': ln2_b[l].reshape(1, -1),
        }
        w_in_bf = w_in[l].astype(BF16)
        lam_p = da_lambda[l]
        sub_g = da_subln_g[l].reshape(1, -1)

        u_p = matmul(xp_bf, w_in_bf)
        o_a = diff_attn_prompt(u_p, lam_p, sub_g, batch, seq, lam_init)
        o_b = moba_prompt(u_p, batch, seq)
        o_c, tail = conv_prompt(u_p, conv_w[l], batch, seq)
        xp, xp_bf = _token_stage(xp, xp_bf, o_a, o_b, o_c, wts)
        u3 = u_p.reshape(batch, seq, IN_WIDTH)
        st_p[0].append(u3[:, :, OFF_KA:OFF_KA + DA_K].reshape(batch, seq, DA_KV_HEADS, 2 * DA_HEAD_DIM))
        st_p[1].append(u3[:, :, OFF_VA:OFF_VA + DA_V].reshape(batch, seq, DA_KV_HEADS, 2 * DA_HEAD_DIM))
        st_p[2].append(u3[:, :, OFF_KB:OFF_KB + MB_W].reshape(batch, seq, MB_HEADS, MB_HEAD_DIM))
        st_p[3].append(u3[:, :, OFF_VB:OFF_VB + MB_W].reshape(batch, seq, MB_HEADS, MB_HEAD_DIM))
        st_p[4].append(tail[:, 8 - (CONV_W - 1):, :])

        u_s = matmul(xs_bf, w_in_bf)[:n_s].reshape(db, t_len, IN_WIDTH)
        seg = lambda off, width: u_s[:, :, off:off + width]
        q_a = seg(OFF_QA, DA_Q).reshape(db, t_len, DA_KV_HEADS, 2, 2, DA_HEAD_DIM)
        q_a = q_a.transpose(0, 2, 4, 3, 1, 5).reshape(db, 4 * DA_KV_HEADS * t_len, DA_HEAD_DIM)
        k_a, v_a = seg(OFF_KA, DA_K), seg(OFF_VA, DA_V)
        new_rows = lambda a: split_component_view(a.reshape(db, t_len, DA_KV_HEADS, 2 * DA_HEAD_DIM)).reshape(
            db, t_len * 2 * DA_KV_HEADS, DA_HEAD_DIM)
        oa_s = diff_attn_sample(q_a, ck_a, cv_a, new_rows(k_a), new_rows(v_a), page_table, lam_p, sub_g,
                                l, lam_init, t_len)
        oa_s = oa_s.reshape(db, DA_KV_HEADS, 2, t_len, 2 * DA_HEAD_DIM).transpose(0, 3, 1, 2, 4)
        oa_s = oa_s.reshape(n_s, DA_Q)

        q_b = seg(OFF_QB, MB_W).reshape(db, t_len, MB_HEADS, MB_HEAD_DIM)
        k_b, v_b = seg(OFF_KB, MB_W), seg(OFF_VB, MB_W)
        head_major = lambda a: a.transpose(0, 2, 1, 3).reshape(db, MB_HEADS * t_len, a.shape[-1])
        token_head = lambda a: a.reshape(db, t_len * MB_HEADS, MB_HEAD_DIM)
        gs = moba_gate_sample(q_b, ck_b, page_table, l)
        ob_s = moba_attn_sample(head_major(q_b), head_major(gs), ck_b, cv_b, token_head(k_b), token_head(v_b),
                                page_table, l)
        ob_s = ob_s.reshape(db, MB_HEADS, t_len, MB_HEAD_DIM).transpose(0, 2, 1, 3).reshape(n_s, MB_W)

        tmaj = lambda off: seg(off, CONV_DIM).transpose(1, 0, 2)
        oc_s, nst = conv_sample(tmaj(OFF_XC), tmaj(OFF_GB), tmaj(OFF_GC), state_conv[l].transpose(1, 0, 2),
                                conv_w[l])
        oc_s = oc_s.transpose(1, 0, 2).reshape(n_s, CONV_DIM)

        pad_tok = lambda a: jnp.pad(a, ((0, pad_s - n_s), (0, 0))).astype(BF16)
        xs, xs_bf = _token_stage(xs, xs_bf, pad_tok(oa_s), pad_tok(ob_s), pad_tok(oc_s), wts)
        st_s[0].append(k_a.reshape(db, t_len, DA_KV_HEADS, 2 * DA_HEAD_DIM))
        st_s[1].append(v_a.reshape(db, t_len, DA_KV_HEADS, 2 * DA_HEAD_DIM))
        st_s[2].append(k_b.reshape(db, t_len, MB_HEADS, MB_HEAD_DIM))
        st_s[3].append(v_b.reshape(db, t_len, MB_HEADS, MB_HEAD_DIM))
        st_s[4].append(nst.transpose(1, 0, 2))

    y_p = xp.reshape(batch, seq, d)
    y_s = xs[:n_s].reshape(db, t_len, d)
    stack = lambda parts: jnp.stack(parts, axis=0)
    return (y_p, y_s, stack(st_p[0]), stack(st_p[1]), stack(st_p[2]), stack(st_p[3]), stack(st_p[4]),
            stack(st_s[0]), stack(st_s[1]), stack(st_s[2]), stack(st_s[3]), stack(st_s[4]))
```

```python
import functools
import math

import jax
import jax.numpy as jnp
from jax import lax
from jax.experimental import pallas as pl
from jax.experimental.pallas import tpu as pltpu

F32 = jnp.float32
BF16 = jnp.bfloat16

D_MODEL = 4096
DEPTH = 2
PAGE_SIZE = 128
DA_HEADS = 8
DA_KV_HEADS = 4
DA_HEAD_DIM = 128
DA_Q = DA_HEADS * 2 * DA_HEAD_DIM
DA_K = DA_KV_HEADS * 2 * DA_HEAD_DIM
DA_V = DA_K
MB_HEADS = 8
MB_HEAD_DIM = 128
MB_W = MB_HEADS * MB_HEAD_DIM
MB_BLOCK = 256
MB_TOPK = 3
CONV_DIM = 1024
CONV_W = 3
IN_WIDTH = DA_Q + DA_K + DA_V + 3 * MB_W + 3 * CONV_DIM
PEER_HEADS = 8
PEER_N_KEYS = 128
PEER_N_EXPERTS = PEER_N_KEYS * PEER_N_KEYS
PEER_KEY_DIM = 256
PEER_TOPK = 16
ALPHA = (2.0 * DEPTH) ** 0.25
LN_EPS = 1e-5
NEG_INF = -1e30

OFF_QA = 0
OFF_KA = DA_Q
OFF_VA = OFF_KA + DA_K
OFF_QB = OFF_VA + DA_V
OFF_KB = OFF_QB + MB_W
OFF_VB = OFF_KB + MB_W
OFF_XC = OFF_VB + MB_W
OFF_GB = OFF_XC + CONV_DIM
OFF_GC = OFF_GB + CONV_DIM

VMEM_LIMIT = 56 * 1024 * 1024

_NT = (((1,), (1,)), ((), ()))
_TN = (((0,), (0,)), ((), ()))


def _cparams(sem):
    return pltpu.CompilerParams(dimension_semantics=sem, vmem_limit_bytes=VMEM_LIMIT)


def _mm_kernel(x_ref, w_ref, o_ref):
    o_ref[...] = jnp.dot(x_ref[...], w_ref[...], preferred_element_type=F32).astype(o_ref.dtype)


def matmul(x, w, out_dtype=F32, tm=1024, tn=512):
    m, k = x.shape
    n = w.shape[1]
    tm = min(tm, m)
    tn = min(tn, n)
    return pl.pallas_call(
        _mm_kernel,
        grid=(m // tm, n // tn),
        in_specs=[pl.BlockSpec((tm, k), lambda i, j: (i, 0)),
                  pl.BlockSpec((k, tn), lambda i, j: (0, j))],
        out_specs=pl.BlockSpec((tm, tn), lambda i, j: (i, j)),
        out_shape=jax.ShapeDtypeStruct((m, n), out_dtype),
        compiler_params=_cparams(("parallel", "arbitrary")),
        name="matmul",
    )(x, w)


def _merge_kernel(x_ref, oa_ref, ob_ref, oc_ref, wga_ref, wgb_ref, wgc_ref,
                  bga_ref, bgb_ref, bgc_ref, wa_ref, wb_ref, wc_ref, o_ref):
    x = x_ref[...]
    pre = [jnp.dot(x, wg_r[...], preferred_element_type=F32) + bg_r[...]
           for wg_r, bg_r in ((wga_ref, bga_ref), (wgb_ref, bgb_ref), (wgc_ref, bgc_ref))]
    br = [jnp.dot(o_r[...], w_r[...], preferred_element_type=F32)
          for o_r, w_r in ((oa_ref, wa_ref), (ob_ref, wb_ref), (oc_ref, wc_ref))]
    gates = [jax.nn.sigmoid(z) for z in pre]
    merged = gates[0] * br[0] + gates[1] * br[1] + gates[2] * br[2]
    o_ref[...] = merged.astype(o_ref.dtype)


def gated_merge(x, o_a, o_b, o_c, w_gate, b_gate, w_a, w_b, w_c, tm=512, tn=256):
    m, d = x.shape
    tm = min(tm, m)
    nj = d // tn
    row = lambda i, j: (i, 0)
    return pl.pallas_call(
        _merge_kernel,
        grid=(m // tm, nj),
        in_specs=[pl.BlockSpec((tm, d), row),
                  pl.BlockSpec((tm, o_a.shape[1]), row),
                  pl.BlockSpec((tm, o_b.shape[1]), row),
                  pl.BlockSpec((tm, o_c.shape[1]), row),
                  pl.BlockSpec((d, tn), lambda i, j: (0, j)),
                  pl.BlockSpec((d, tn), lambda i, j: (0, j + nj)),
                  pl.BlockSpec((d, tn), lambda i, j: (0, j + 2 * nj)),
                  pl.BlockSpec((1, tn), lambda i, j: (0, j)),
                  pl.BlockSpec((1, tn), lambda i, j: (0, j + nj)),
                  pl.BlockSpec((1, tn), lambda i, j: (0, j + 2 * nj)),
                  pl.BlockSpec((w_a.shape[0], tn), lambda i, j: (0, j)),
                  pl.BlockSpec((w_b.shape[0], tn), lambda i, j: (0, j)),
                  pl.BlockSpec((w_c.shape[0], tn), lambda i, j: (0, j))],
        out_specs=pl.BlockSpec((tm, tn), lambda i, j: (i, j)),
        out_shape=jax.ShapeDtypeStruct((m, d), BF16),
        compiler_params=_cparams(("parallel", "arbitrary")),
        name="gated_merge",
    )(x, o_a, o_b, o_c, w_gate, w_gate, w_gate, b_gate, b_gate, b_gate, w_a, w_b, w_c)


def _layer_norm(z, g, b):
    mu = jnp.mean(z, axis=-1, keepdims=True)
    zc = z - mu
    var = jnp.mean(zc * zc, axis=-1, keepdims=True)
    return zc * lax.rsqrt(var + LN_EPS) * g + b


def _mm_ln_kernel(a_ref, w_ref, res_ref, g_ref, b_ref, o_ref, obf_ref):
    k = pl.program_id(1)

    @pl.when(k == 0)
    def _():
        o_ref[...] = jnp.zeros_like(o_ref)

    o_ref[...] += jnp.dot(a_ref[...], w_ref[...], preferred_element_type=F32)

    @pl.when(k == pl.num_programs(1) - 1)
    def _():
        y = _layer_norm(ALPHA * res_ref[...] + o_ref[...], g_ref[...], b_ref[...])
        o_ref[...] = y
        obf_ref[...] = y.astype(BF16)


def matmul_residual_ln(a, w, res, g, b, tm=512, tk=512):
    m, kdim = a.shape
    d = w.shape[1]
    tm = min(tm, m)
    return pl.pallas_call(
        _mm_ln_kernel,
        grid=(m // tm, kdim // tk),
        in_specs=[pl.BlockSpec((tm, tk), lambda i, k: (i, k)),
                  pl.BlockSpec((tk, d), lambda i, k: (k, 0)),
                  pl.BlockSpec((tm, d), lambda i, k: (i, 0), pipeline_mode=pl.Buffered(1)),
                  pl.BlockSpec((1, d), lambda i, k: (0, 0)),
                  pl.BlockSpec((1, d), lambda i, k: (0, 0))],
        out_specs=[pl.BlockSpec((tm, d), lambda i, k: (i, 0)),
                   pl.BlockSpec((tm, d), lambda i, k: (i, 0))],
        out_shape=[jax.ShapeDtypeStruct((m, d), F32), jax.ShapeDtypeStruct((m, d), BF16)],
        compiler_params=_cparams(("parallel", "arbitrary")),
        name="matmul_residual_ln",
    )(a, w, res, g, b)


def _add_ln_kernel(res_ref, f_ref, g_ref, b_ref, o_ref, obf_ref):
    y = _layer_norm(ALPHA * res_ref[...] + f_ref[...], g_ref[...], b_ref[...])
    o_ref[...] = y
    obf_ref[...] = y.astype(BF16)


def add_ln(res, f, g, b, tm=256):
    m, d = res.shape
    tm = min(tm, m)
    spec = pl.BlockSpec((tm, d), lambda i: (i, 0))
    vec = pl.BlockSpec((1, d), lambda i: (0, 0))
    return pl.pallas_call(
        _add_ln_kernel,
        grid=(m // tm,),
        in_specs=[spec, spec, vec, vec],
        out_specs=[spec, spec],
        out_shape=[jax.ShapeDtypeStruct((m, d), F32), jax.ShapeDtypeStruct((m, d), BF16)],
        compiler_params=_cparams(("parallel",)),
        name="add_ln",
    )(res, f, g, b)


def _lambda_value(lp, lam_init):
    a = jnp.sum(lp[0:1, :] * lp[1:2, :], axis=-1, keepdims=True)
    b = jnp.sum(lp[2:3, :] * lp[3:4, :], axis=-1, keepdims=True)
    return jnp.exp(a) - jnp.exp(b) + lam_init


def _online_update(s, v_bf, m_ref, l_ref, acc_ref):
    m_prev = m_ref[...]
    m_new = jnp.maximum(m_prev, jnp.max(s, axis=-1, keepdims=True))
    alpha = jnp.exp(m_prev - m_new)
    p = jnp.exp(s - m_new)
    l_ref[...] = alpha * l_ref[...] + jnp.sum(p, axis=-1, keepdims=True)
    acc_ref[...] = alpha * acc_ref[...] + jnp.dot(p.astype(BF16), v_bf, preferred_element_type=F32)
    m_ref[...] = m_new


def _subln(o, g, lam_init):
    return o * lax.rsqrt(jnp.mean(o * o, axis=-1, keepdims=True) + LN_EPS) * g * (1.0 - lam_init)


def _online_update_keymajor(scores, vts, m_ref, l_ref, acc_ref):
    chains = range(len(scores))
    m_prev = [m_ref[c] for c in chains]
    m_new = [jnp.maximum(m_prev[c], jnp.max(scores[c], axis=0, keepdims=True)) for c in chains]
    alpha = [jnp.exp(m_prev[c] - m_new[c]) for c in chains]
    p = [jnp.exp(scores[c] - m_new[c]) for c in chains]
    for c in chains:
        l_ref[c] = alpha[c] * l_ref[c] + jnp.sum(p[c], axis=0, keepdims=True)
        m_ref[c] = m_new[c]
    pv = [jnp.dot(vts[c], p[c].astype(BF16), preferred_element_type=F32) for c in chains]
    for c in chains:
        acc_ref[c] = alpha[c] * acc_ref[c] + pv[c]


def _diff_prompt_kernel(q_ref, k_ref, v_ref, lam_ref, g_ref, o_ref, vt_ref, m_ref, l_ref, acc_ref, *,
                        lam_init, tq):
    i = pl.program_id(2)
    dh = DA_HEAD_DIM
    scale = dh ** -0.5
    n_kv = v_ref.shape[0] // tq

    @pl.when(i == 0)
    def _():
        for j in range(n_kv):
            vt_ref[j] = v_ref[j * tq:(j + 1) * tq, :].T.astype(BF16)

    lam = _lambda_value(lam_ref[...], lam_init)
    key = lax.broadcasted_iota(jnp.int32, (tq, 2 * tq), 0)
    qry = lax.broadcasted_iota(jnp.int32, (tq, 2 * tq), 1)
    causal = key <= jnp.where(qry >= tq, qry - tq, qry)
    qs = [(jnp.concatenate([q_ref[:, c * dh:(c + 1) * dh], q_ref[:, (2 + c) * dh:(3 + c) * dh]], axis=0)
           * scale).astype(BF16) for c in range(2)]
    m_ref[...] = jnp.full_like(m_ref, NEG_INF)
    l_ref[...] = jnp.zeros_like(l_ref)
    acc_ref[...] = jnp.zeros_like(acc_ref)

    def tile(j, masked):
        start = pl.multiple_of(j * tq, tq)
        vt = vt_ref[j]
        scores = []
        for c in range(2):
            kt = k_ref[pl.ds(start, tq), c * dh:(c + 1) * dh].astype(BF16)
            s = lax.dot_general(kt, qs[c], _NT, preferred_element_type=F32)
            scores.append(jnp.where(causal, s, NEG_INF) if masked else s)
        _online_update_keymajor(scores, [vt, vt], m_ref, l_ref, acc_ref)

    def body(j, carry):
        tile(j, False)
        return carry

    lax.fori_loop(0, i, body, 0)
    tile(i, True)
    ot = acc_ref[0] / l_ref[0] - lam * (acc_ref[1] / l_ref[1])
    ot = ot * lax.rsqrt(jnp.mean(ot * ot, axis=0, keepdims=True) + LN_EPS) * g_ref[...] * (1.0 - lam_init)
    o = ot.T.astype(o_ref.dtype)
    o_ref[:, 0:2 * dh] = o[:tq]
    o_ref[:, 2 * dh:4 * dh] = o[tq:]


def diff_attn_prompt(u, lam_p, subln_g, batch, seq, lam_init, tq=512):
    nq = seq // tq
    qw = 4 * DA_HEAD_DIM
    kw = 2 * DA_HEAD_DIM
    kern = functools.partial(_diff_prompt_kernel, lam_init=lam_init, tq=tq)
    return pl.pallas_call(
        kern,
        grid=(batch, DA_KV_HEADS, nq),
        in_specs=[pl.BlockSpec((tq, qw), lambda b, k, i: (b * nq + i, k)),
                  pl.BlockSpec((seq, kw), lambda b, k, i: (b, OFF_KA // kw + k)),
                  pl.BlockSpec((seq, kw), lambda b, k, i: (b, OFF_VA // kw + k)),
                  pl.BlockSpec((4, DA_HEAD_DIM), lambda b, k, i: (0, 0)),
                  pl.BlockSpec((kw, 1), lambda b, k, i: (0, 0))],
        out_specs=pl.BlockSpec((tq, qw), lambda b, k, i: (b * nq + i, k)),
        out_shape=jax.ShapeDtypeStruct((batch * seq, DA_Q), BF16),
        scratch_shapes=[pltpu.VMEM((nq, kw, tq), BF16),
                        pltpu.VMEM((2, 1, 2 * tq), F32), pltpu.VMEM((2, 1, 2 * tq), F32),
                        pltpu.VMEM((2, kw, 2 * tq), F32)],
        compiler_params=_cparams(("parallel", "parallel", "arbitrary")),
        name="diff_attn_prompt",
    )(u, u, u, lam_p, subln_g.reshape(kw, 1))


def _moba_prompt_kernel(q_ref, k_ref, v_ref, o_ref, km_ref, vt_ref, sel_ref, m_ref, l_ref, acc_ref, *, nb, hg):
    i = pl.program_id(2)
    blk = MB_BLOCK
    dh = MB_HEAD_DIM
    scale = dh ** -0.5

    @pl.when(i == 0)
    def _():
        km_ref[...] = jnp.zeros_like(km_ref)
        for n in range(nb):
            km_ref[n:n + 1, :] = jnp.mean(k_ref[n * blk:(n + 1) * blk, :], axis=0, keepdims=True)
            for h in range(hg):
                vt_ref[h, n] = v_ref[n * blk:(n + 1) * blk, h * dh:(h + 1) * dh].T.astype(BF16)

    heads = range(hg)
    qf = [q_ref[:, h * dh:(h + 1) * dh] for h in heads]
    qs = [(qf[h] * scale).astype(BF16) for h in heads]
    gs = [lax.dot_general(km_ref[:, h * dh:(h + 1) * dh], qf[h], _NT, preferred_element_type=F32,
                          precision=lax.Precision.HIGHEST) for h in heads]
    block = lax.broadcasted_iota(jnp.int32, gs[0].shape, 0)
    past = block < i
    for n in range(nb):
        ahead = [(gs[h] > gs[h][n:n + 1, :]) | ((gs[h] == gs[h][n:n + 1, :]) & (block < n)) for h in heads]
        rank = [jnp.sum(jnp.where(past & ahead[h], 1.0, 0.0), axis=0, keepdims=True) for h in heads]
        for h in heads:
            sel_ref[h, n] = jnp.where(rank[h] < MB_TOPK, 1.0, 0.0)

    key = lax.broadcasted_iota(jnp.int32, (blk, blk), 0)
    qry = lax.broadcasted_iota(jnp.int32, (blk, blk), 1)
    m_ref[...] = jnp.full_like(m_ref, NEG_INF)
    l_ref[...] = jnp.zeros_like(l_ref)
    acc_ref[...] = jnp.zeros_like(acc_ref)

    def tile(n, mask_of):
        start = pl.multiple_of(n * blk, blk)
        scores = []
        for h in range(hg):
            kt = k_ref[pl.ds(start, blk), h * dh:(h + 1) * dh].astype(BF16)
            s = lax.dot_general(kt, qs[h], _NT, preferred_element_type=F32)
            scores.append(jnp.where(mask_of(h), s, NEG_INF))
        _online_update_keymajor(scores, [vt_ref[h, n] for h in range(hg)], m_ref, l_ref, acc_ref)

    tile(i, lambda h: key <= qry)

    def body(n, carry):
        tile(n, lambda h: sel_ref[h, n] > 0.5)
        return carry

    lax.fori_loop(0, i, body, 0)
    for h in range(hg):
        o_ref[:, h * dh:(h + 1) * dh] = (acc_ref[h] / l_ref[h]).T.astype(o_ref.dtype)


def moba_prompt(u, batch, seq, hg=MB_HEADS):
    nb = seq // MB_BLOCK
    dh = MB_HEAD_DIM
    w = hg * dh
    kern = functools.partial(_moba_prompt_kernel, nb=nb, hg=hg)
    once = pl.Buffered(1)
    return pl.pallas_call(
        kern,
        grid=(batch, MB_HEADS // hg, nb),
        in_specs=[pl.BlockSpec((MB_BLOCK, w), lambda b, h, i: (b * nb + i, OFF_QB // w + h)),
                  pl.BlockSpec((seq, w), lambda b, h, i: (b, OFF_KB // w + h), pipeline_mode=once),
                  pl.BlockSpec((seq, w), lambda b, h, i: (b, OFF_VB // w + h), pipeline_mode=once)],
        out_specs=pl.BlockSpec((MB_BLOCK, w), lambda b, h, i: (b * nb + i, h)),
        out_shape=jax.ShapeDtypeStruct((batch * seq, MB_W), BF16),
        scratch_shapes=[pltpu.VMEM((-(-nb // 8) * 8, w), F32), pltpu.VMEM((hg, nb, dh, MB_BLOCK), BF16),
                        pltpu.VMEM((hg, nb, 1, MB_BLOCK), F32),
                        pltpu.VMEM((hg, 1, MB_BLOCK), F32), pltpu.VMEM((hg, 1, MB_BLOCK), F32),
                        pltpu.VMEM((hg, dh, MB_BLOCK), F32)],
        compiler_params=_cparams(("parallel", "parallel", "arbitrary")),
        name="moba_prompt",
    )(u, u, u)


def _conv_prompt_kernel(xc_ref, gb_ref, gc_ref, w_ref, o_ref, st_ref):
    z = gc_ref[...] * xc_ref[...]
    t = z.shape[0]
    row = lax.broadcasted_iota(jnp.int32, z.shape, 0)
    z1 = jnp.where(row >= 1, pltpu.roll(z, 1, 0), 0.0)
    z2 = jnp.where(row >= 2, pltpu.roll(z, 2, 0), 0.0)
    y = w_ref[0:1, :] * z2 + w_ref[1:2, :] * z1 + w_ref[2:3, :] * z
    o_ref[...] = (gb_ref[...] * y).astype(o_ref.dtype)
    st_ref[0] = z[t - 8:, :]


def conv_prompt(u, conv_w, batch, seq, tc=256):
    nc = CONV_DIM // tc
    blk = lambda off: pl.BlockSpec((seq, tc), lambda b, c: (b, off // tc + c))
    return pl.pallas_call(
        _conv_prompt_kernel,
        grid=(batch, nc),
        in_specs=[blk(OFF_XC), blk(OFF_GB), blk(OFF_GC), pl.BlockSpec((CONV_W, tc), lambda b, c: (0, c))],
        out_specs=[pl.BlockSpec((seq, tc), lambda b, c: (b, c)),
                   pl.BlockSpec((1, 8, tc), lambda b, c: (b, 0, c))],
        out_shape=[jax.ShapeDtypeStruct((batch * seq, CONV_DIM), BF16),
                   jax.ShapeDtypeStruct((batch, 8, CONV_DIM), F32)],
        compiler_params=_cparams(("parallel", "parallel")),
        name="conv_prompt",
    )(u, u, u, conv_w)


def _conv_sample_kernel(xc_ref, gb_ref, gc_ref, st_ref, w_ref, o_ref, nst_ref):
    t_len = xc_ref.shape[0]
    zz = [st_ref[0], st_ref[1]] + [gc_ref[t] * xc_ref[t] for t in range(t_len)]
    for t in range(t_len):
        y = w_ref[0:1, :] * zz[t] + w_ref[1:2, :] * zz[t + 1] + w_ref[2:3, :] * zz[t + 2]
        o_ref[t] = gb_ref[t] * y
    nst_ref[0] = zz[t_len]
    nst_ref[1] = zz[t_len + 1]


def conv_sample(xc, gb, gc, state, conv_w):
    return pl.pallas_call(
        _conv_sample_kernel,
        out_shape=[jax.ShapeDtypeStruct(xc.shape, F32), jax.ShapeDtypeStruct(state.shape, F32)],
        name="conv_sample",
    )(xc, gb, gc, state, conv_w)


def _page_specs(layer, pg, heads, dh):
    return [pl.BlockSpec((None, None, PAGE_SIZE, heads, dh),
                         lambda b, p, pt, j=j: (layer, pt[b, p * pg + j], 0, 0, 0)) for j in range(pg)]


def _token_head_rows(page_refs, dtype=BF16):
    ps, heads, dh = page_refs[0].shape
    return jnp.concatenate([r[...].reshape(ps * heads, dh).astype(dtype) for r in page_refs], axis=0)


def split_component_view(x):
    lead = x.shape[:-2]
    kv, w = x.shape[-2:]
    n = len(lead)
    x = x.reshape(*lead, kv, 2, w // 2)
    return x.transpose(*range(n), n + 1, n, n + 2).reshape(*lead, 2 * kv, w // 2)


def _swap_row_halves(x):
    n, d = x.shape
    return pltpu.roll(x.reshape(n // 8, 8, d), 4, 1).reshape(n, d)


def _diff_sample_kernel(pt_ref, q_ref, *refs, lam_init, t_len, pg):
    k_refs, v_refs = refs[:pg], refs[pg:2 * pg]
    kn_ref, vn_ref, lam_ref, g_ref, o_ref, m_ref, l_ref, acc_ref = refs[2 * pg:]
    p = pl.program_id(1)
    dh = DA_HEAD_DIM
    kv = DA_KV_HEADS
    scale = dh ** -0.5
    per_comp = 2 * t_len
    rows = kv * 2 * per_comp
    q = q_ref[0].astype(BF16)

    def own_rows(n_keys):
        r = lax.broadcasted_iota(jnp.int32, (rows, n_keys), 0)
        j = lax.broadcasted_iota(jnp.int32, (rows, n_keys), 1)
        target = ((r // per_comp) % 2) * kv + r // (2 * per_comp)
        return (j % (2 * kv)) == target, r, j

    def values(v_f32):
        return jnp.concatenate([v_f32, _swap_row_halves(v_f32)], axis=-1).astype(BF16)

    @pl.when(p == 0)
    def _():
        m_ref[...] = jnp.full_like(m_ref, NEG_INF)
        l_ref[...] = jnp.zeros_like(l_ref)
        acc_ref[...] = jnp.zeros_like(acc_ref)

    s = lax.dot_general(q, _token_head_rows(k_refs), _NT, preferred_element_type=F32) * scale
    own, _, _ = own_rows(s.shape[1])
    s = jnp.where(own, s, NEG_INF)
    _online_update(s, values(_token_head_rows(v_refs, F32)), m_ref, l_ref, acc_ref)

    @pl.when(p == pl.num_programs(1) - 1)
    def _():
        sn = lax.dot_general(q, kn_ref[0].astype(BF16), _NT, preferred_element_type=F32) * scale
        own_n, r, j = own_rows(sn.shape[1])
        sn = jnp.where(own_n & ((j // (2 * kv)) <= (r % t_len)), sn, NEG_INF)
        _online_update(sn, values(vn_ref[0]), m_ref, l_ref, acc_ref)
        o = acc_ref[...] / l_ref[...]
        lam = _lambda_value(lam_ref[...], lam_init)
        for k in range(kv):
            base = k * 2 * per_comp
            o1 = o[base:base + per_comp, :]
            o2 = o[base + per_comp:base + 2 * per_comp, :]
            o2 = jnp.concatenate([o2[:, dh:], o2[:, :dh]], axis=-1)
            o_ref[0, k] = _subln(o1 - lam * o2, g_ref[...], lam_init)


def diff_attn_sample(q, cache_k, cache_v, k_new, v_new, page_table, lam_p, subln_g, layer, lam_init, t_len,
                     pg=8):
    db, n_pages = page_table.shape
    rows, dh = q.shape[1:]
    per_comp = 2 * t_len
    kern = functools.partial(_diff_sample_kernel, lam_init=lam_init, t_len=t_len, pg=pg)
    pages = _page_specs(layer, pg, 2 * DA_KV_HEADS, dh)
    per_b = lambda shape: pl.BlockSpec(shape, lambda b, p, pt: (b, 0, 0))
    n_new = k_new.shape[1]
    grid_spec = pltpu.PrefetchScalarGridSpec(
        num_scalar_prefetch=1,
        grid=(db, n_pages // pg),
        in_specs=[per_b((1, rows, dh))] + pages + pages + [
            per_b((1, n_new, dh)), per_b((1, n_new, dh)),
            pl.BlockSpec((4, dh), lambda b, p, pt: (0, 0)),
            pl.BlockSpec((1, 2 * dh), lambda b, p, pt: (0, 0))],
        out_specs=pl.BlockSpec((1, DA_KV_HEADS, per_comp, 2 * dh), lambda b, p, pt: (b, 0, 0, 0)),
        scratch_shapes=[pltpu.VMEM((rows, 1), F32), pltpu.VMEM((rows, 1), F32), pltpu.VMEM((rows, 2 * dh), F32)],
    )
    return pl.pallas_call(
        kern,
        grid_spec=grid_spec,
        out_shape=jax.ShapeDtypeStruct((db, DA_KV_HEADS, per_comp, 2 * dh), F32),
        compiler_params=_cparams(("parallel", "arbitrary")),
        name="diff_attn_sample",
    )(page_table, q, *([cache_k] * pg), *([cache_v] * pg), k_new, v_new, lam_p, subln_g)


def _moba_gate_kernel(pt_ref, q_ref, *refs, pg, t_len):
    k_refs, gs_ref = refs[:pg], refs[pg]
    p = pl.program_id(1)
    pages_per_block = MB_BLOCK // PAGE_SIZE
    blocks = pg // pages_per_block

    @pl.when(p == 0)
    def _():
        gs_ref[...] = jnp.zeros_like(gs_ref)

    lane = lax.broadcasted_iota(jnp.int32, gs_ref.shape[2:], 1)
    for blk in range(blocks):
        page_sums = [jnp.sum(k_refs[blk * pages_per_block + j][...], axis=0) for j in range(pages_per_block)]
        km = functools.reduce(lambda a, b: a + b, page_sums) / MB_BLOCK
        n = p * blocks + blk
        for t in range(t_len):
            g = jnp.sum(q_ref[0, t] * km, axis=-1, keepdims=True)
            gs_ref[0, t] += jnp.where(lane == n, g, 0.0)


def moba_gate_sample(q, cache_k, page_table, layer, pg=8):
    db, n_pages = page_table.shape
    t_len = q.shape[1]
    kern = functools.partial(_moba_gate_kernel, pg=pg, t_len=t_len)
    grid_spec = pltpu.PrefetchScalarGridSpec(
        num_scalar_prefetch=1,
        grid=(db, n_pages // pg),
        in_specs=[pl.BlockSpec((1, t_len, MB_HEADS, MB_HEAD_DIM), lambda b, p, pt: (b, 0, 0, 0))]
        + _page_specs(layer, pg, MB_HEADS, MB_HEAD_DIM),
        out_specs=pl.BlockSpec((1, t_len, MB_HEADS, 128), lambda b, p, pt: (b, 0, 0, 0)),
    )
    return pl.pallas_call(
        kern,
        grid_spec=grid_spec,
        out_shape=jax.ShapeDtypeStruct((db, t_len, MB_HEADS, 128), F32),
        compiler_params=_cparams(("parallel", "arbitrary")),
        name="moba_gate_sample",
    )(page_table, q, *([cache_k] * pg))


def _moba_sample_kernel(pt_ref, q_ref, gs_ref, *refs, t_len, n_blocks, pg):
    k_refs, v_refs = refs[:pg], refs[pg:2 * pg]
    kn_ref, vn_ref, o_ref, sel_ref, m_ref, l_ref, acc_ref = refs[2 * pg:]
    p = pl.program_id(1)
    scale = MB_HEAD_DIM ** -0.5
    heads = MB_HEADS
    rows = heads * t_len
    block_keys = MB_BLOCK * heads
    blocks = pg * PAGE_SIZE // MB_BLOCK
    q = q_ref[0].astype(BF16)
    lane = lax.broadcasted_iota(jnp.int32, (rows, 128), 1).astype(F32)

    def head_match(n_keys):
        r = lax.broadcasted_iota(jnp.int32, (rows, n_keys), 0)
        j = lax.broadcasted_iota(jnp.int32, (rows, n_keys), 1)
        return (j % heads) == (r // t_len), r, j

    @pl.when(p == 0)
    def _():
        cur = jnp.where(lane < n_blocks, gs_ref[0], -jnp.inf)
        sel = jnp.zeros((rows, 128), F32)
        for _ in range(MB_TOPK):
            mx = jnp.max(cur, axis=-1, keepdims=True)
            first = jnp.min(jnp.where(cur == mx, lane, 128.0), axis=-1, keepdims=True)
            hit = lane == first
            sel = jnp.where(hit, 1.0, sel)
            cur = jnp.where(hit, -jnp.inf, cur)
        sel_ref[...] = sel
        m_ref[...] = jnp.full_like(m_ref, NEG_INF)
        l_ref[...] = jnp.zeros_like(l_ref)
        acc_ref[...] = jnp.zeros_like(acc_ref)
        sn = lax.dot_general(q, kn_ref[0].astype(BF16), _NT, preferred_element_type=F32) * scale
        same_head, r, j = head_match(sn.shape[1])
        sn = jnp.where(same_head & ((j // heads) <= (r % t_len)), sn, NEG_INF)
        _online_update(sn, vn_ref[0].astype(BF16), m_ref, l_ref, acc_ref)

    sel = sel_ref[...]
    chosen = []
    for blk in range(blocks):
        block = (p * blocks + blk).astype(F32)
        hit = jnp.sum(jnp.where(lane == block, sel, 0.0), axis=-1, keepdims=True)
        chosen.append(jnp.broadcast_to(hit, (rows, block_keys)))
    s = lax.dot_general(q, _token_head_rows(k_refs), _NT, preferred_element_type=F32) * scale
    same_head, _, _ = head_match(s.shape[1])
    s = jnp.where(same_head & (jnp.concatenate(chosen, axis=-1) > 0.5), s, NEG_INF)
    _online_update(s, _token_head_rows(v_refs), m_ref, l_ref, acc_ref)

    @pl.when(p == pl.num_programs(1) - 1)
    def _():
        o_ref[0] = acc_ref[...] / l_ref[...]


def moba_attn_sample(q, gs, cache_k, cache_v, k_new, v_new, page_table, layer, pg=8):
    db, n_pages = page_table.shape
    rows, dh = q.shape[1:]
    t_len = rows // MB_HEADS
    n_blocks = n_pages * PAGE_SIZE // MB_BLOCK
    kern = functools.partial(_moba_sample_kernel, t_len=t_len, n_blocks=n_blocks, pg=pg)
    pages = _page_specs(layer, pg, MB_HEADS, MB_HEAD_DIM)
    per_b = lambda shape: pl.BlockSpec(shape, lambda b, p, pt: (b, 0, 0))
    grid_spec = pltpu.PrefetchScalarGridSpec(
        num_scalar_prefetch=1,
        grid=(db, n_pages // pg),
        in_specs=[per_b((1, rows, dh)), per_b((1, rows, 128))] + pages + pages + [
            per_b((1, rows, dh)), per_b((1, rows, dh))],
        out_specs=per_b((1, rows, dh)),
        scratch_shapes=[pltpu.VMEM((rows, 128), F32), pltpu.VMEM((rows, 1), F32), pltpu.VMEM((rows, 1), F32),
                        pltpu.VMEM((rows, dh), F32)],
    )
    return pl.pallas_call(
        kern,
        grid_spec=grid_spec,
        out_shape=jax.ShapeDtypeStruct((db, rows, dh), F32),
        compiler_params=_cparams(("parallel", "arbitrary")),
        name="moba_attn_sample",
    )(page_table, q, gs, *([cache_k] * pg), *([cache_v] * pg), k_new, v_new)


def _extract_topk(cur_ref, out_ref, k):
    groups = cur_ref.shape[0]

    def body(j, carry):
        cur = [cur_ref[g] for g in range(groups)]
        mx = [jnp.max(cur[g], axis=0, keepdims=True) for g in range(groups)]
        for g in range(groups):
            out_ref[g, pl.ds(j, 1), :] = mx[g]
            cur_ref[g] = jnp.where(cur[g] == mx[g], -jnp.inf, cur[g])
        return carry

    lax.fori_loop(0, k, body, 0)


_CAND_ROWS = PEER_TOPK + 7 * 8 + 8


def _peer_route_kernel(q_ref, sk_ref, s1m_ref, s2m_ref, eb_ref, a0_ref, thr_ref,
                       cur_ref, top_ref, cand_ref, best_ref):
    half = PEER_KEY_DIM // 2
    k = PEER_TOPK
    for h in range(PEER_HEADS):
        for c, raw_ref in enumerate((s1m_ref, s2m_ref)):
            qc = q_ref[:, (h * 2 + c) * half:(h * 2 + c + 1) * half]
            s = lax.dot_general(sk_ref[h, c], qc, _NT, preferred_element_type=F32,
                                precision=lax.Precision.HIGHEST)
            raw_ref[h] = s
            cur_ref[h * 2 + c] = s
    _extract_topk(cur_ref, top_ref, k)
    for h in range(PEER_HEADS):
        ta = top_ref[2 * h]
        tb = top_ref[2 * h + 1]
        cand_ref[h, 0:k, :] = ta[0:1, :] + tb
        for p in range(1, 8):
            cand_ref[h, k + (p - 1) * 8:k + p * 8, :] = ta[p:p + 1, :] + tb[0:8, :]
        cand_ref[h, k + 56:k + 64, :] = ta[8:16, :] + tb[0:1, :]
    _extract_topk(cand_ref, best_ref, k)
    for h in range(PEER_HEADS):
        ta = top_ref[2 * h]
        tb = top_ref[2 * h + 1]
        best = best_ref[h]
        z = jnp.sum(jnp.exp(best - best[0:1, :]), axis=0, keepdims=True)
        s1 = s1m_ref[h]
        s2 = s2m_ref[h]
        s1m_ref[h] = jnp.where(s1 >= ta[k - 1:k, :], s1, NEG_INF)
        s2m_ref[h] = jnp.where(s2 >= tb[k - 1:k, :], s2, NEG_INF)
        eb_ref[h] = jnp.exp(s2 - tb[0:1, :]) / z
        a0_ref[h:h + 1, :] = ta[0:1, :]
        thr_ref[h:h + 1, :] = best[k - 1:k, :]


def peer_route(q, sub_keys, tm=256):
    m = q.shape[0]
    tm = min(tm, m)
    nk = PEER_N_KEYS
    big = pl.BlockSpec((PEER_HEADS, nk, tm), lambda i: (0, 0, i))
    small = pl.BlockSpec((PEER_HEADS, tm), lambda i: (0, i))
    big_shape = jax.ShapeDtypeStruct((PEER_HEADS, nk, m), F32)
    small_shape = jax.ShapeDtypeStruct((PEER_HEADS, m), F32)
    return pl.pallas_call(
        _peer_route_kernel,
        grid=(m // tm,),
        in_specs=[pl.BlockSpec((tm, PEER_HEADS * PEER_KEY_DIM), lambda i: (i, 0)),
                  pl.BlockSpec(sub_keys.shape, lambda i: (0, 0, 0, 0))],
        out_specs=[big, big, big, small, small],
        out_shape=[big_shape, big_shape, big_shape, small_shape, small_shape],
        scratch_shapes=[pltpu.VMEM((2 * PEER_HEADS, nk, tm), F32),
                        pltpu.VMEM((2 * PEER_HEADS, PEER_TOPK, tm), F32),
                        pltpu.VMEM((PEER_HEADS, _CAND_ROWS, tm), F32),
                        pltpu.VMEM((PEER_HEADS, PEER_TOPK, tm), F32)],
        compiler_params=_cparams(("parallel",)),
        name="peer_route",
    )(q, sub_keys)


def _peer_expert_kernel(x_ref, u_ref, v_ref, s1m_ref, s2m_ref, eb_ref, a0_ref, thr_ref, o_ref, a_ref, *, te):
    e = pl.program_id(1)
    nk = PEER_N_KEYS

    @pl.when(e == 0)
    def _():
        o_ref[...] = jnp.zeros_like(o_ref)

    g = lax.dot_general(u_ref[...], x_ref[...], _NT, preferred_element_type=F32)
    for ii in range(te // nk):
        i = e * (te // nk) + ii
        w = jnp.zeros((nk, g.shape[1]), F32)
        for h in range(PEER_HEADS):
            s1 = s1m_ref[h, pl.ds(i, 1), :]
            ea = jnp.exp(s1 - a0_ref[h:h + 1, :])
            hit = (s1 + s2m_ref[h]) >= thr_ref[h:h + 1, :]
            w = w + jnp.where(hit, ea * eb_ref[h], 0.0)
        gi = g[ii * nk:(ii + 1) * nk, :]
        act = 0.5 * gi * (1.0 + lax.erf(gi * (2.0 ** -0.5)))
        a_ref[ii * nk:(ii + 1) * nk, :] = (w * act).astype(BF16)
    o_ref[...] += lax.dot_general(a_ref[...], v_ref[...], _TN, preferred_element_type=F32)


def peer_experts(x, u, v, s1m, s2m, eb, a0, thr, tm=512, te=512):
    m, d = x.shape
    tm = min(tm, m)
    n_e = u.shape[0]
    kern = functools.partial(_peer_expert_kernel, te=te)
    once = pl.Buffered(1)
    big = pl.BlockSpec((PEER_HEADS, PEER_N_KEYS, tm), lambda i, e: (0, 0, i), pipeline_mode=once)
    small = pl.BlockSpec((PEER_HEADS, tm), lambda i, e: (0, i), pipeline_mode=once)
    return pl.pallas_call(
        kern,
        grid=(m // tm, n_e // te),
        in_specs=[pl.BlockSpec((tm, d), lambda i, e: (i, 0), pipeline_mode=once),
                  pl.BlockSpec((te, d), lambda i, e: (e, 0)),
                  pl.BlockSpec((te, d), lambda i, e: (e, 0)),
                  big, big, big, small, small],
        out_specs=pl.BlockSpec((tm, d), lambda i, e: (i, 0)),
        out_shape=jax.ShapeDtypeStruct((m, d), F32),
        scratch_shapes=[pltpu.VMEM((te, tm), BF16)],
        compiler_params=_cparams(("parallel", "arbitrary")),
        name="peer_experts",
    )(x, u, v, s1m, s2m, eb, a0, thr)


def _token_stage(x, x_bf, o_a, o_b, o_c, wts):
    merged = gated_merge(x_bf, o_a, o_b, o_c, wts['w_gate'], wts['b_gate'], wts['w_br_a'], wts['w_br_b'],
                         wts['w_br_c'])
    h, h_bf = matmul_residual_ln(merged, wts['w_o'], x, wts['ln1_g'], wts['ln1_b'])
    q = matmul(h_bf, wts['peer_wq'])
    s1m, s2m, eb, a0, thr = peer_route(q, wts['peer_subkeys'])
    f = peer_experts(h_bf, wts['peer_u'], wts['peer_v'], s1m, s2m, eb, a0, thr)
    return add_ln(h, f, wts['ln2_g'], wts['ln2_b'])


def kernel(x_prompt, x_sample, cache_diff_k, cache_diff_v, cache_moba_k, cache_moba_v, state_conv, page_table,
           w_in, da_lambda, da_subln_g, conv_w, w_gate, b_gate, w_br_a, w_br_b, w_br_c, w_o, ln1_g, ln1_b,
           peer_wq, peer_subkeys, peer_u, peer_v, ln2_g, ln2_b):
    batch, seq, d = x_prompt.shape
    db, t_len, _ = x_sample.shape
    n_p = batch * seq
    n_s = db * t_len
    pad_s = 128
    ck_a, cv_a = split_component_view(cache_diff_k), split_component_view(cache_diff_v)
    ck_b, cv_b = cache_moba_k, cache_moba_v

    xp = x_prompt.reshape(n_p, d)
    xs = jnp.pad(x_sample.reshape(n_s, d), ((0, pad_s - n_s), (0, 0)))
    xp_bf = xp.astype(BF16)
    xs_bf = xs.astype(BF16)

    st_p = [[] for _ in range(5)]
    st_s = [[] for _ in range(5)]
    for l in range(DEPTH):
        lam_init = 0.8 - 0.6 * math.exp(-0.3 * l)
        wts = {
            'w_gate': w_gate[l].astype(BF16), 'b_gate': b_gate[l].reshape(1, -1),
            'w_br_a': w_br_a[l].astype(BF16), 'w_br_b': w_br_b[l].astype(BF16), 'w_br_c': w_br_c[l].astype(BF16),
            'w_o': w_o[l].astype(BF16), 'ln1_g': ln1_g[l].reshape(1, -1), 'ln1_b': ln1_b[l].reshape(1, -1),
            'peer_wq': peer_wq[l].astype(BF16), 'peer_subkeys': peer_subkeys[l],
            'peer_u': peer_u[l].astype(BF16), 'peer_v': peer_v[l].astype(BF16),
            'ln2_g': ln2_g[l].reshape(1, -1), 'ln2_b': ln2_b[l].reshape(1, -1),
        }
        w_in_bf = w_in[l].astype(BF16)
        lam_p = da_lambda[l]
        sub_g = da_subln_g[l].reshape(1, -1)

        u_p = matmul(xp_bf, w_in_bf)
        o_a = diff_attn_prompt(u_p, lam_p, sub_g, batch, seq, lam_init)
        o_b = moba_prompt(u_p, batch, seq)
        o_c, tail = conv_prompt(u_p, conv_w[l], batch, seq)
        xp, xp_bf = _token_stage(xp, xp_bf, o_a, o_b, o_c, wts)
        u3 = u_p.reshape(batch, seq, IN_WIDTH)
        st_p[0].append(u3[:, :, OFF_KA:OFF_KA + DA_K].reshape(batch, seq, DA_KV_HEADS, 2 * DA_HEAD_DIM))
        st_p[1].append(u3[:, :, OFF_VA:OFF_VA + DA_V].reshape(batch, seq, DA_KV_HEADS, 2 * DA_HEAD_DIM))
        st_p[2].append(u3[:, :, OFF_KB:OFF_KB + MB_W].reshape(batch, seq, MB_HEADS, MB_HEAD_DIM))
        st_p[3].append(u3[:, :, OFF_VB:OFF_VB + MB_W].reshape(batch, seq, MB_HEADS, MB_HEAD_DIM))
        st_p[4].append(tail[:, 8 - (CONV_W - 1):, :])

        u_s = matmul(xs_bf, w_in_bf)[:n_s].reshape(db, t_len, IN_WIDTH)
        seg = lambda off, width: u_s[:, :, off:off + width]
        q_a = seg(OFF_QA, DA_Q).reshape(db, t_len, DA_KV_HEADS, 2, 2, DA_HEAD_DIM)
        q_a = q_a.transpose(0, 2, 4, 3, 1, 5).reshape(db, 4 * DA_KV_HEADS * t_len, DA_HEAD_DIM)
        k_a, v_a = seg(OFF_KA, DA_K), seg(OFF_VA, DA_V)
        new_rows = lambda a: split_component_view(a.reshape(db, t_len, DA_KV_HEADS, 2 * DA_HEAD_DIM)).reshape(
            db, t_len * 2 * DA_KV_HEADS, DA_HEAD_DIM)
        oa_s = diff_attn_sample(q_a, ck_a, cv_a, new_rows(k_a), new_rows(v_a), page_table, lam_p, sub_g,
                                l, lam_init, t_len)
        oa_s = oa_s.reshape(db, DA_KV_HEADS, 2, t_len, 2 * DA_HEAD_DIM).transpose(0, 3, 1, 2, 4)
        oa_s = oa_s.reshape(n_s, DA_Q)

        q_b = seg(OFF_QB, MB_W).reshape(db, t_len, MB_HEADS, MB_HEAD_DIM)
        k_b, v_b = seg(OFF_KB, MB_W), seg(OFF_VB, MB_W)
        head_major = lambda a: a.transpose(0, 2, 1, 3).reshape(db, MB_HEADS * t_len, a.shape[-1])
        token_head = lambda a: a.reshape(db, t_len * MB_HEADS, MB_HEAD_DIM)
        gs = moba_gate_sample(q_b, ck_b, page_table, l)
        ob_s = moba_attn_sample(head_major(q_b), head_major(gs), ck_b, cv_b, token_head(k_b), token_head(v_b),
                                page_table, l)
        ob_s = ob_s.reshape(db, MB_HEADS, t_len, MB_HEAD_DIM).transpose(0, 2, 1, 3).reshape(n_s, MB_W)

        tmaj = lambda off: seg(off, CONV_DIM).transpose(1, 0, 2)
        oc_s, nst = conv_sample(tmaj(OFF_XC), tmaj(OFF_GB), tmaj(OFF_GC), state_conv[l].transpose(1, 0, 2),
                                conv_w[l])
        oc_s = oc_s.transpose(1, 0, 2).reshape(n_s, CONV_DIM)

        pad_tok = lambda a: jnp.pad(a, ((0, pad_s - n_s), (0, 0))).astype(BF16)
        xs, xs_bf = _token_stage(xs, xs_bf, pad_tok(oa_s), pad_tok(ob_s), pad_tok(oc_s), wts)
        st_s[0].append(k_a.reshape(db, t_len, DA_KV_HEADS, 2 * DA_HEAD_DIM))
        st_s[1].append(v_a.reshape(db, t_len, DA_KV_HEADS, 2 * DA_HEAD_DIM))
        st_s[2].append(k_b.reshape(db, t_len, MB_HEADS, MB_HEAD_DIM))
        st_s[3].append(v_b.reshape(db, t_len, MB_HEADS, MB_HEAD_DIM))
        st_s[4].append(nst.transpose(1, 0, 2))

    y_p = xp.reshape(batch, seq, d)
    y_s = xs[:n_s].reshape(db, t_len, d)
    stack = lambda parts: jnp.stack(parts, axis=0)
    return (y_p, y_s, stack(st_p[0]), stack(st_p[1]), stack(st_p[2]), stack(st_p[3]), stack(st_p[4]),
            stack(st_s[0]), stack(st_s[1]), stack(st_s[2]), stack(st_s[3]), stack(st_s[4]))
```

```python
import functools
import math

import jax
import jax.numpy as jnp
from jax import lax
from jax.experimental import pallas as pl
from jax.experimental.pallas import tpu as pltpu

F32 = jnp.float32
BF16 = jnp.bfloat16

D_MODEL = 4096
DEPTH = 2
PAGE_SIZE = 128
DA_HEADS = 8
DA_KV_HEADS = 4
DA_HEAD_DIM = 128
DA_Q = DA_HEADS * 2 * DA_HEAD_DIM
DA_K = DA_KV_HEADS * 2 * DA_HEAD_DIM
DA_V = DA_K
MB_HEADS = 8
MB_HEAD_DIM = 128
MB_W = MB_HEADS * MB_HEAD_DIM
MB_BLOCK = 256
MB_TOPK = 3
CONV_DIM = 1024
CONV_W = 3
IN_WIDTH = DA_Q + DA_K + DA_V + 3 * MB_W + 3 * CONV_DIM
PEER_HEADS = 8
PEER_N_KEYS = 128
PEER_N_EXPERTS = PEER_N_KEYS * PEER_N_KEYS
PEER_KEY_DIM = 256
PEER_TOPK = 16
ALPHA = (2.0 * DEPTH) ** 0.25
LN_EPS = 1e-5
NEG_INF = -1e30

OFF_QA = 0
OFF_KA = DA_Q
OFF_VA = OFF_KA + DA_K
OFF_QB = OFF_VA + DA_V
OFF_KB = OFF_QB + MB_W
OFF_VB = OFF_KB + MB_W
OFF_XC = OFF_VB + MB_W
OFF_GB = OFF_XC + CONV_DIM
OFF_GC = OFF_GB + CONV_DIM

VMEM_LIMIT = 56 * 1024 * 1024

_NT = (((1,), (1,)), ((), ()))
_TN = (((0,), (0,)), ((), ()))


def _cparams(sem):
    return pltpu.CompilerParams(dimension_semantics=sem, vmem_limit_bytes=VMEM_LIMIT)


_ONCE = pl.Buffered(1)


def _mm_kernel(x_ref, w_ref, o_ref, wbf_ref):
    @pl.when(pl.program_id(1) == 0)
    def _():
        wbf_ref[...] = w_ref[...].astype(BF16)

    o_ref[...] = jnp.dot(x_ref[...], wbf_ref[...], preferred_element_type=F32).astype(o_ref.dtype)


def matmul(x, w, layer, out_dtype=F32, tm=1024, tn=512):
    m, k = x.shape
    n = w.shape[2]
    tm = min(tm, m)
    tn = min(tn, n)
    return pl.pallas_call(
        _mm_kernel,
        grid=(n // tn, m // tm),
        in_specs=[pl.BlockSpec((tm, k), lambda j, i: (i, 0)),
                  pl.BlockSpec((None, k, tn), lambda j, i: (layer, 0, j), pipeline_mode=_ONCE)],
        out_specs=pl.BlockSpec((tm, tn), lambda j, i: (i, j)),
        out_shape=jax.ShapeDtypeStruct((m, n), out_dtype),
        scratch_shapes=[pltpu.VMEM((k, tn), BF16)],
        compiler_params=_cparams(("parallel", "arbitrary")),
        name="matmul",
    )(x, w)


def _merge_kernel(x_ref, oa_ref, ob_ref, oc_ref, wga_ref, wgb_ref, wgc_ref,
                  bga_ref, bgb_ref, bgc_ref, wa_ref, wb_ref, wc_ref, o_ref,
                  wg_bf, wa_bf, wb_bf, wc_bf):
    @pl.when(pl.program_id(1) == 0)
    def _():
        for b, w_r in enumerate((wga_ref, wgb_ref, wgc_ref)):
            wg_bf[b] = w_r[...].astype(BF16)
        wa_bf[...] = wa_ref[...].astype(BF16)
        wb_bf[...] = wb_ref[...].astype(BF16)
        wc_bf[...] = wc_ref[...].astype(BF16)

    x = x_ref[...]
    pre = [jnp.dot(x, wg_bf[b], preferred_element_type=F32) + bg_r[...]
           for b, bg_r in enumerate((bga_ref, bgb_ref, bgc_ref))]
    br = [jnp.dot(o_r[...], w_r[...], preferred_element_type=F32)
          for o_r, w_r in ((oa_ref, wa_bf), (ob_ref, wb_bf), (oc_ref, wc_bf))]
    gates = [jax.nn.sigmoid(z) for z in pre]
    merged = gates[0] * br[0] + gates[1] * br[1] + gates[2] * br[2]
    o_ref[...] = merged.astype(o_ref.dtype)


def gated_merge(x, o_a, o_b, o_c, w_gate, b_gate, w_a, w_b, w_c, layer, tm=512, tn=256):
    m, d = x.shape
    tm = min(tm, m)
    nj = d // tn
    row = lambda j, i: (i, 0)
    panel = lambda rows, off: pl.BlockSpec((None, rows, tn), lambda j, i: (layer, 0, j + off), pipeline_mode=_ONCE)
    bias = lambda off: pl.BlockSpec((None, 1, tn), lambda j, i: (layer, 0, j + off))
    return pl.pallas_call(
        _merge_kernel,
        grid=(nj, m // tm),
        in_specs=[pl.BlockSpec((tm, d), row),
                  pl.BlockSpec((tm, o_a.shape[1]), row),
                  pl.BlockSpec((tm, o_b.shape[1]), row),
                  pl.BlockSpec((tm, o_c.shape[1]), row),
                  panel(d, 0), panel(d, nj), panel(d, 2 * nj),
                  bias(0), bias(nj), bias(2 * nj),
                  panel(w_a.shape[1], 0), panel(w_b.shape[1], 0), panel(w_c.shape[1], 0)],
        out_specs=pl.BlockSpec((tm, tn), lambda j, i: (i, j)),
        out_shape=jax.ShapeDtypeStruct((m, d), BF16),
        scratch_shapes=[pltpu.VMEM((3, d, tn), BF16), pltpu.VMEM((w_a.shape[1], tn), BF16),
                        pltpu.VMEM((w_b.shape[1], tn), BF16), pltpu.VMEM((w_c.shape[1], tn), BF16)],
        compiler_params=_cparams(("parallel", "arbitrary")),
        name="gated_merge",
    )(x, o_a, o_b, o_c, w_gate, w_gate, w_gate, b_gate, b_gate, b_gate, w_a, w_b, w_c)


def _layer_norm(z, g, b):
    mu = jnp.mean(z, axis=-1, keepdims=True)
    zc = z - mu
    var = jnp.mean(zc * zc, axis=-1, keepdims=True)
    return zc * lax.rsqrt(var + LN_EPS) * g + b


def _mm_ln_kernel(a_ref, w_ref, res_ref, g_ref, b_ref, o_ref, obf_ref):
    k = pl.program_id(1)

    @pl.when(k == 0)
    def _():
        o_ref[...] = jnp.zeros_like(o_ref)

    o_ref[...] += jnp.dot(a_ref[...], w_ref[...], preferred_element_type=F32)

    @pl.when(k == pl.num_programs(1) - 1)
    def _():
        y = _layer_norm(ALPHA * res_ref[...] + o_ref[...], g_ref[...], b_ref[...])
        o_ref[...] = y
        obf_ref[...] = y.astype(BF16)


def matmul_residual_ln(a, w, layer, res, g, b, tm=512, tk=512):
    m, kdim = a.shape
    d = w.shape[2]
    tm = min(tm, m)
    return pl.pallas_call(
        _mm_ln_kernel,
        grid=(m // tm, kdim // tk),
        in_specs=[pl.BlockSpec((tm, tk), lambda i, k: (i, k)),
                  pl.BlockSpec((None, tk, d), lambda i, k: (layer, k, 0)),
                  pl.BlockSpec((tm, d), lambda i, k: (i, 0), pipeline_mode=pl.Buffered(1)),
                  pl.BlockSpec((1, d), lambda i, k: (0, 0)),
                  pl.BlockSpec((1, d), lambda i, k: (0, 0))],
        out_specs=[pl.BlockSpec((tm, d), lambda i, k: (i, 0)),
                   pl.BlockSpec((tm, d), lambda i, k: (i, 0))],
        out_shape=[jax.ShapeDtypeStruct((m, d), F32), jax.ShapeDtypeStruct((m, d), BF16)],
        compiler_params=_cparams(("parallel", "arbitrary")),
        name="matmul_residual_ln",
    )(a, w, res, g, b)


def _add_ln_kernel(res_ref, f_ref, g_ref, b_ref, o_ref, obf_ref):
    y = _layer_norm(ALPHA * res_ref[...] + f_ref[...], g_ref[...], b_ref[...])
    o_ref[...] = y
    obf_ref[...] = y.astype(BF16)


def add_ln(res, f, g, b, tm=256):
    m, d = res.shape
    tm = min(tm, m)
    spec = pl.BlockSpec((tm, d), lambda i: (i, 0))
    vec = pl.BlockSpec((1, d), lambda i: (0, 0))
    return pl.pallas_call(
        _add_ln_kernel,
        grid=(m // tm,),
        in_specs=[spec, spec, vec, vec],
        out_specs=[spec, spec],
        out_shape=[jax.ShapeDtypeStruct((m, d), F32), jax.ShapeDtypeStruct((m, d), BF16)],
        compiler_params=_cparams(("parallel",)),
        name="add_ln",
    )(res, f, g, b)


def _lambda_value(lp, lam_init):
    a = jnp.sum(lp[0:1, :] * lp[1:2, :], axis=-1, keepdims=True)
    b = jnp.sum(lp[2:3, :] * lp[3:4, :], axis=-1, keepdims=True)
    return jnp.exp(a) - jnp.exp(b) + lam_init


def _online_update(s, v_bf, m_ref, l_ref, acc_ref):
    m_prev = m_ref[...]
    m_new = jnp.maximum(m_prev, jnp.max(s, axis=-1, keepdims=True))
    alpha = jnp.exp(m_prev - m_new)
    p = jnp.exp(s - m_new)
    l_ref[...] = alpha * l_ref[...] + jnp.sum(p, axis=-1, keepdims=True)
    acc_ref[...] = alpha * acc_ref[...] + jnp.dot(p.astype(BF16), v_bf, preferred_element_type=F32)
    m_ref[...] = m_new


def _subln(o, g, lam_init):
    return o * lax.rsqrt(jnp.mean(o * o, axis=-1, keepdims=True) + LN_EPS) * g * (1.0 - lam_init)


def _online_update_keymajor(scores, vts, m_ref, l_ref, acc_ref):
    chains = range(len(scores))
    m_prev = [m_ref[c] for c in chains]
    m_new = [jnp.maximum(m_prev[c], jnp.max(scores[c], axis=0, keepdims=True)) for c in chains]
    alpha = [jnp.exp(m_prev[c] - m_new[c]) for c in chains]
    p = [jnp.exp(scores[c] - m_new[c]) for c in chains]
    for c in chains:
        l_ref[c] = alpha[c] * l_ref[c] + jnp.sum(p[c], axis=0, keepdims=True)
        m_ref[c] = m_new[c]
    pv = [jnp.dot(vts[c], p[c].astype(BF16), preferred_element_type=F32) for c in chains]
    for c in chains:
        acc_ref[c] = alpha[c] * acc_ref[c] + pv[c]


def _diff_prompt_kernel(q_ref, k_ref, v_ref, lam_ref, g_ref, o_ref, vt_ref, m_ref, l_ref, acc_ref, *,
                        lam_init, tq):
    i = pl.program_id(2)
    dh = DA_HEAD_DIM
    scale = dh ** -0.5
    n_kv = v_ref.shape[0] // tq

    @pl.when(i == 0)
    def _():
        for j in range(n_kv):
            vt_ref[j] = v_ref[j * tq:(j + 1) * tq, :].T.astype(BF16)

    lam = _lambda_value(lam_ref[...], lam_init)
    key = lax.broadcasted_iota(jnp.int32, (tq, 2 * tq), 0)
    qry = lax.broadcasted_iota(jnp.int32, (tq, 2 * tq), 1)
    causal = key <= jnp.where(qry >= tq, qry - tq, qry)
    qs = [(jnp.concatenate([q_ref[:, c * dh:(c + 1) * dh], q_ref[:, (2 + c) * dh:(3 + c) * dh]], axis=0)
           * scale).astype(BF16) for c in range(2)]
    m_ref[...] = jnp.full_like(m_ref, NEG_INF)
    l_ref[...] = jnp.zeros_like(l_ref)
    acc_ref[...] = jnp.zeros_like(acc_ref)

    def tile(j, masked):
        start = pl.multiple_of(j * tq, tq)
        vt = vt_ref[j]
        scores = []
        for c in range(2):
            kt = k_ref[pl.ds(start, tq), c * dh:(c + 1) * dh].astype(BF16)
            s = lax.dot_general(kt, qs[c], _NT, preferred_element_type=F32)
            scores.append(jnp.where(causal, s, NEG_INF) if masked else s)
        _online_update_keymajor(scores, [vt, vt], m_ref, l_ref, acc_ref)

    def body(j, carry):
        tile(j, False)
        return carry

    lax.fori_loop(0, i, body, 0)
    tile(i, True)
    ot = acc_ref[0] / l_ref[0] - lam * (acc_ref[1] / l_ref[1])
    ot = ot * lax.rsqrt(jnp.mean(ot * ot, axis=0, keepdims=True) + LN_EPS) * g_ref[...] * (1.0 - lam_init)
    o = ot.T.astype(o_ref.dtype)
    o_ref[:, 0:2 * dh] = o[:tq]
    o_ref[:, 2 * dh:4 * dh] = o[tq:]


def diff_attn_prompt(u, lam_p, subln_g, batch, seq, lam_init, tq=512):
    nq = seq // tq
    qw = 4 * DA_HEAD_DIM
    kw = 2 * DA_HEAD_DIM
    kern = functools.partial(_diff_prompt_kernel, lam_init=lam_init, tq=tq)
    return pl.pallas_call(
        kern,
        grid=(batch, DA_KV_HEADS, nq),
        in_specs=[pl.BlockSpec((tq, qw), lambda b, k, i: (b * nq + i, k)),
                  pl.BlockSpec((seq, kw), lambda b, k, i: (b, OFF_KA // kw + k)),
                  pl.BlockSpec((seq, kw), lambda b, k, i: (b, OFF_VA // kw + k)),
                  pl.BlockSpec((4, DA_HEAD_DIM), lambda b, k, i: (0, 0)),
                  pl.BlockSpec((kw, 1), lambda b, k, i: (0, 0))],
        out_specs=pl.BlockSpec((tq, qw), lambda b, k, i: (b * nq + i, k)),
        out_shape=jax.ShapeDtypeStruct((batch * seq, DA_Q), BF16),
        scratch_shapes=[pltpu.VMEM((nq, kw, tq), BF16),
                        pltpu.VMEM((2, 1, 2 * tq), F32), pltpu.VMEM((2, 1, 2 * tq), F32),
                        pltpu.VMEM((2, kw, 2 * tq), F32)],
        compiler_params=_cparams(("parallel", "parallel", "arbitrary")),
        name="diff_attn_prompt",
    )(u, u, u, lam_p, subln_g.reshape(kw, 1))


def _moba_prompt_kernel(q_ref, k_ref, v_ref, o_ref, km_ref, vt_ref, sel_ref, m_ref, l_ref, acc_ref, *, nb, hg):
    i = pl.program_id(2)
    blk = MB_BLOCK
    dh = MB_HEAD_DIM
    scale = dh ** -0.5

    @pl.when(i == 0)
    def _():
        km_ref[...] = jnp.zeros_like(km_ref)
        for n in range(nb):
            km_ref[n:n + 1, :] = jnp.mean(k_ref[n * blk:(n + 1) * blk, :], axis=0, keepdims=True)
            for h in range(hg):
                vt_ref[h, n] = v_ref[n * blk:(n + 1) * blk, h * dh:(h + 1) * dh].T.astype(BF16)

    heads = range(hg)
    qf = [q_ref[:, h * dh:(h + 1) * dh] for h in heads]
    qs = [(qf[h] * scale).astype(BF16) for h in heads]
    gs = [lax.dot_general(km_ref[:, h * dh:(h + 1) * dh], qf[h], _NT, preferred_element_type=F32,
                          precision=lax.Precision.HIGHEST) for h in heads]
    block = lax.broadcasted_iota(jnp.int32, gs[0].shape, 0)
    past = block < i
    for n in range(nb):
        ahead = [(gs[h] > gs[h][n:n + 1, :]) | ((gs[h] == gs[h][n:n + 1, :]) & (block < n)) for h in heads]
        rank = [jnp.sum(jnp.where(past & ahead[h], 1.0, 0.0), axis=0, keepdims=True) for h in heads]
        for h in heads:
            sel_ref[h, n] = jnp.where(rank[h] < MB_TOPK, 1.0, 0.0)

    key = lax.broadcasted_iota(jnp.int32, (blk, blk), 0)
    qry = lax.broadcasted_iota(jnp.int32, (blk, blk), 1)
    m_ref[...] = jnp.full_like(m_ref, NEG_INF)
    l_ref[...] = jnp.zeros_like(l_ref)
    acc_ref[...] = jnp.zeros_like(acc_ref)

    def tile(n, mask_of):
        start = pl.multiple_of(n * blk, blk)
        scores = []
        for h in range(hg):
            kt = k_ref[pl.ds(start, blk), h * dh:(h + 1) * dh].astype(BF16)
            s = lax.dot_general(kt, qs[h], _NT, preferred_element_type=F32)
            scores.append(jnp.where(mask_of(h), s, NEG_INF))
        _online_update_keymajor(scores, [vt_ref[h, n] for h in range(hg)], m_ref, l_ref, acc_ref)

    tile(i, lambda h: key <= qry)

    def body(n, carry):
        tile(n, lambda h: sel_ref[h, n] > 0.5)
        return carry

    lax.fori_loop(0, i, body, 0)
    for h in range(hg):
        o_ref[:, h * dh:(h + 1) * dh] = (acc_ref[h] / l_ref[h]).T.astype(o_ref.dtype)


def moba_prompt(u, batch, seq, hg=MB_HEADS):
    nb = seq // MB_BLOCK
    dh = MB_HEAD_DIM
    w = hg * dh
    kern = functools.partial(_moba_prompt_kernel, nb=nb, hg=hg)
    once = pl.Buffered(1)
    return pl.pallas_call(
        kern,
        grid=(batch, MB_HEADS // hg, nb),
        in_specs=[pl.BlockSpec((MB_BLOCK, w), lambda b, h, i: (b * nb + i, OFF_QB // w + h)),
                  pl.BlockSpec((seq, w), lambda b, h, i: (b, OFF_KB // w + h), pipeline_mode=once),
                  pl.BlockSpec((seq, w), lambda b, h, i: (b, OFF_VB // w + h), pipeline_mode=once)],
        out_specs=pl.BlockSpec((MB_BLOCK, w), lambda b, h, i: (b * nb + i, h)),
        out_shape=jax.ShapeDtypeStruct((batch * seq, MB_W), BF16),
        scratch_shapes=[pltpu.VMEM((-(-nb // 8) * 8, w), F32), pltpu.VMEM((hg, nb, dh, MB_BLOCK), BF16),
                        pltpu.VMEM((hg, nb, 1, MB_BLOCK), F32),
                        pltpu.VMEM((hg, 1, MB_BLOCK), F32), pltpu.VMEM((hg, 1, MB_BLOCK), F32),
                        pltpu.VMEM((hg, dh, MB_BLOCK), F32)],
        compiler_params=_cparams(("parallel", "parallel", "arbitrary")),
        name="moba_prompt",
    )(u, u, u)


def _conv_prompt_kernel(xc_ref, gb_ref, gc_ref, w_ref, o_ref, st_ref):
    z = gc_ref[...] * xc_ref[...]
    t = z.shape[0]
    row = lax.broadcasted_iota(jnp.int32, z.shape, 0)
    z1 = jnp.where(row >= 1, pltpu.roll(z, 1, 0), 0.0)
    z2 = jnp.where(row >= 2, pltpu.roll(z, 2, 0), 0.0)
    y = w_ref[0:1, :] * z2 + w_ref[1:2, :] * z1 + w_ref[2:3, :] * z
    o_ref[...] = (gb_ref[...] * y).astype(o_ref.dtype)
    st_ref[0] = z[t - 8:, :]


def conv_prompt(u, conv_w, batch, seq, tc=256):
    nc = CONV_DIM // tc
    blk = lambda off: pl.BlockSpec((seq, tc), lambda b, c: (b, off // tc + c))
    return pl.pallas_call(
        _conv_prompt_kernel,
        grid=(batch, nc),
        in_specs=[blk(OFF_XC), blk(OFF_GB), blk(OFF_GC), pl.BlockSpec((CONV_W, tc), lambda b, c: (0, c))],
        out_specs=[pl.BlockSpec((seq, tc), lambda b, c: (b, c)),
                   pl.BlockSpec((1, 8, tc), lambda b, c: (b, 0, c))],
        out_shape=[jax.ShapeDtypeStruct((batch * seq, CONV_DIM), BF16),
                   jax.ShapeDtypeStruct((batch, 8, CONV_DIM), F32)],
        compiler_params=_cparams(("parallel", "parallel")),
        name="conv_prompt",
    )(u, u, u, conv_w)


def _conv_sample_kernel(xc_ref, gb_ref, gc_ref, st_ref, w_ref, o_ref, nst_ref):
    t_len = xc_ref.shape[0]
    zz = [st_ref[0], st_ref[1]] + [gc_ref[t] * xc_ref[t] for t in range(t_len)]
    for t in range(t_len):
        y = w_ref[0:1, :] * zz[t] + w_ref[1:2, :] * zz[t + 1] + w_ref[2:3, :] * zz[t + 2]
        o_ref[t] = gb_ref[t] * y
    nst_ref[0] = zz[t_len]
    nst_ref[1] = zz[t_len + 1]


def conv_sample(xc, gb, gc, state, conv_w):
    return pl.pallas_call(
        _conv_sample_kernel,
        out_shape=[jax.ShapeDtypeStruct(xc.shape, F32), jax.ShapeDtypeStruct(state.shape, F32)],
        name="conv_sample",
    )(xc, gb, gc, state, conv_w)


def _page_specs(layer, pg, heads, dh):
    return [pl.BlockSpec((None, None, PAGE_SIZE, heads, dh),
                         lambda b, p, pt, j=j: (layer, pt[b, p * pg + j], 0, 0, 0)) for j in range(pg)]


def _token_head_rows(page_refs, dtype=BF16):
    ps, heads, dh = page_refs[0].shape
    return jnp.concatenate([r[...].reshape(ps * heads, dh).astype(dtype) for r in page_refs], axis=0)


def split_component_view(x):
    lead = x.shape[:-2]
    kv, w = x.shape[-2:]
    n = len(lead)
    x = x.reshape(*lead, kv, 2, w // 2)
    return x.transpose(*range(n), n + 1, n, n + 2).reshape(*lead, 2 * kv, w // 2)


def _swap_row_halves(x):
    n, d = x.shape
    return pltpu.roll(x.reshape(n // 8, 8, d), 4, 1).reshape(n, d)


def _diff_sample_kernel(pt_ref, q_ref, *refs, lam_init, t_len, pg):
    k_refs, v_refs = refs[:pg], refs[pg:2 * pg]
    kn_ref, vn_ref, lam_ref, g_ref, o_ref, m_ref, l_ref, acc_ref = refs[2 * pg:]
    p = pl.program_id(1)
    dh = DA_HEAD_DIM
    kv = DA_KV_HEADS
    scale = dh ** -0.5
    per_comp = 2 * t_len
    rows = kv * 2 * per_comp
    q = q_ref[0].astype(BF16)

    def own_rows(n_keys):
        r = lax.broadcasted_iota(jnp.int32, (rows, n_keys), 0)
        j = lax.broadcasted_iota(jnp.int32, (rows, n_keys), 1)
        target = ((r // per_comp) % 2) * kv + r // (2 * per_comp)
        return (j % (2 * kv)) == target, r, j

    def values(v_f32):
        return jnp.concatenate([v_f32, _swap_row_halves(v_f32)], axis=-1).astype(BF16)

    @pl.when(p == 0)
    def _():
        m_ref[...] = jnp.full_like(m_ref, NEG_INF)
        l_ref[...] = jnp.zeros_like(l_ref)
        acc_ref[...] = jnp.zeros_like(acc_ref)

    s = lax.dot_general(q, _token_head_rows(k_refs), _NT, preferred_element_type=F32) * scale
    own, _, _ = own_rows(s.shape[1])
    s = jnp.where(own, s, NEG_INF)
    _online_update(s, values(_token_head_rows(v_refs, F32)), m_ref, l_ref, acc_ref)

    @pl.when(p == pl.num_programs(1) - 1)
    def _():
        sn = lax.dot_general(q, kn_ref[0].astype(BF16), _NT, preferred_element_type=F32) * scale
        own_n, r, j = own_rows(sn.shape[1])
        sn = jnp.where(own_n & ((j // (2 * kv)) <= (r % t_len)), sn, NEG_INF)
        _online_update(sn, values(vn_ref[0]), m_ref, l_ref, acc_ref)
        o = acc_ref[...] / l_ref[...]
        lam = _lambda_value(lam_ref[...], lam_init)
        for k in range(kv):
            base = k * 2 * per_comp
            o1 = o[base:base + per_comp, :]
            o2 = o[base + per_comp:base + 2 * per_comp, :]
            o2 = jnp.concatenate([o2[:, dh:], o2[:, :dh]], axis=-1)
            o_ref[0, k] = _subln(o1 - lam * o2, g_ref[...], lam_init)


def diff_attn_sample(q, cache_k, cache_v, k_new, v_new, page_table, lam_p, subln_g, layer, lam_init, t_len,
                     pg=8):
    db, n_pages = page_table.shape
    rows, dh = q.shape[1:]
    per_comp = 2 * t_len
    kern = functools.partial(_diff_sample_kernel, lam_init=lam_init, t_len=t_len, pg=pg)
    pages = _page_specs(layer, pg, 2 * DA_KV_HEADS, dh)
    per_b = lambda shape: pl.BlockSpec(shape, lambda b, p, pt: (b, 0, 0))
    n_new = k_new.shape[1]
    grid_spec = pltpu.PrefetchScalarGridSpec(
        num_scalar_prefetch=1,
        grid=(db, n_pages // pg),
        in_specs=[per_b((1, rows, dh))] + pages + pages + [
            per_b((1, n_new, dh)), per_b((1, n_new, dh)),
            pl.BlockSpec((4, dh), lambda b, p, pt: (0, 0)),
            pl.BlockSpec((1, 2 * dh), lambda b, p, pt: (0, 0))],
        out_specs=pl.BlockSpec((1, DA_KV_HEADS, per_comp, 2 * dh), lambda b, p, pt: (b, 0, 0, 0)),
        scratch_shapes=[pltpu.VMEM((rows, 1), F32), pltpu.VMEM((rows, 1), F32), pltpu.VMEM((rows, 2 * dh), F32)],
    )
    return pl.pallas_call(
        kern,
        grid_spec=grid_spec,
        out_shape=jax.ShapeDtypeStruct((db, DA_KV_HEADS, per_comp, 2 * dh), F32),
        compiler_params=_cparams(("parallel", "arbitrary")),
        name="diff_attn_sample",
    )(page_table, q, *([cache_k] * pg), *([cache_v] * pg), k_new, v_new, lam_p, subln_g)


def _moba_gate_kernel(pt_ref, q_ref, *refs, pg, t_len):
    k_refs, gs_ref = refs[:pg], refs[pg]
    p = pl.program_id(1)
    pages_per_block = MB_BLOCK // PAGE_SIZE
    blocks = pg // pages_per_block

    @pl.when(p == 0)
    def _():
        gs_ref[...] = jnp.zeros_like(gs_ref)

    lane = lax.broadcasted_iota(jnp.int32, gs_ref.shape[2:], 1)
    for blk in range(blocks):
        page_sums = [jnp.sum(k_refs[blk * pages_per_block + j][...], axis=0) for j in range(pages_per_block)]
        km = functools.reduce(lambda a, b: a + b, page_sums) / MB_BLOCK
        n = p * blocks + blk
        for t in range(t_len):
            g = jnp.sum(q_ref[0, t] * km, axis=-1, keepdims=True)
            gs_ref[0, t] += jnp.where(lane == n, g, 0.0)


def moba_gate_sample(q, cache_k, page_table, layer, pg=8):
    db, n_pages = page_table.shape
    t_len = q.shape[1]
    kern = functools.partial(_moba_gate_kernel, pg=pg, t_len=t_len)
    grid_spec = pltpu.PrefetchScalarGridSpec(
        num_scalar_prefetch=1,
        grid=(db, n_pages // pg),
        in_specs=[pl.BlockSpec((1, t_len, MB_HEADS, MB_HEAD_DIM), lambda b, p, pt: (b, 0, 0, 0))]
        + _page_specs(layer, pg, MB_HEADS, MB_HEAD_DIM),
        out_specs=pl.BlockSpec((1, t_len, MB_HEADS, 128), lambda b, p, pt: (b, 0, 0, 0)),
    )
    return pl.pallas_call(
        kern,
        grid_spec=grid_spec,
        out_shape=jax.ShapeDtypeStruct((db, t_len, MB_HEADS, 128), F32),
        compiler_params=_cparams(("parallel", "arbitrary")),
        name="moba_gate_sample",
    )(page_table, q, *([cache_k] * pg))


def _moba_sample_kernel(pt_ref, q_ref, gs_ref, *refs, t_len, n_blocks, pg):
    k_refs, v_refs = refs[:pg], refs[pg:2 * pg]
    kn_ref, vn_ref, o_ref, sel_ref, m_ref, l_ref, acc_ref = refs[2 * pg:]
    p = pl.program_id(1)
    scale = MB_HEAD_DIM ** -0.5
    heads = MB_HEADS
    rows = heads * t_len
    block_keys = MB_BLOCK * heads
    blocks = pg * PAGE_SIZE // MB_BLOCK
    q = q_ref[0].astype(BF16)
    lane = lax.broadcasted_iota(jnp.int32, (rows, 128), 1).astype(F32)

    def head_match(n_keys):
        r = lax.broadcasted_iota(jnp.int32, (rows, n_keys), 0)
        j = lax.broadcasted_iota(jnp.int32, (rows, n_keys), 1)
        return (j % heads) == (r // t_len), r, j

    @pl.when(p == 0)
    def _():
        cur = jnp.where(lane < n_blocks, gs_ref[0], -jnp.inf)
        sel = jnp.zeros((rows, 128), F32)
        for _ in range(MB_TOPK):
            mx = jnp.max(cur, axis=-1, keepdims=True)
            first = jnp.min(jnp.where(cur == mx, lane, 128.0), axis=-1, keepdims=True)
            hit = lane == first
            sel = jnp.where(hit, 1.0, sel)
            cur = jnp.where(hit, -jnp.inf, cur)
        sel_ref[...] = sel
        m_ref[...] = jnp.full_like(m_ref, NEG_INF)
        l_ref[...] = jnp.zeros_like(l_ref)
        acc_ref[...] = jnp.zeros_like(acc_ref)
        sn = lax.dot_general(q, kn_ref[0].astype(BF16), _NT, preferred_element_type=F32) * scale
        same_head, r, j = head_match(sn.shape[1])
        sn = jnp.where(same_head & ((j // heads) <= (r % t_len)), sn, NEG_INF)
        _online_update(sn, vn_ref[0].astype(BF16), m_ref, l_ref, acc_ref)

    sel = sel_ref[...]
    chosen = []
    for blk in range(blocks):
        block = (p * blocks + blk).astype(F32)
        hit = jnp.sum(jnp.where(lane == block, sel, 0.0), axis=-1, keepdims=True)
        chosen.append(jnp.broadcast_to(hit, (rows, block_keys)))
    s = lax.dot_general(q, _token_head_rows(k_refs), _NT, preferred_element_type=F32) * scale
    same_head, _, _ = head_match(s.shape[1])
    s = jnp.where(same_head & (jnp.concatenate(chosen, axis=-1) > 0.5), s, NEG_INF)
    _online_update(s, _token_head_rows(v_refs), m_ref, l_ref, acc_ref)

    @pl.when(p == pl.num_programs(1) - 1)
    def _():
        o_ref[0] = acc_ref[...] / l_ref[...]


def moba_attn_sample(q, gs, cache_k, cache_v, k_new, v_new, page_table, layer, pg=8):
    db, n_pages = page_table.shape
    rows, dh = q.shape[1:]
    t_len = rows // MB_HEADS
    n_blocks = n_pages * PAGE_SIZE // MB_BLOCK
    kern = functools.partial(_moba_sample_kernel, t_len=t_len, n_blocks=n_blocks, pg=pg)
    pages = _page_specs(layer, pg, MB_HEADS, MB_HEAD_DIM)
    per_b = lambda shape: pl.BlockSpec(shape, lambda b, p, pt: (b, 0, 0))
    grid_spec = pltpu.PrefetchScalarGridSpec(
        num_scalar_prefetch=1,
        grid=(db, n_pages // pg),
        in_specs=[per_b((1, rows, dh)), per_b((1, rows, 128))] + pages + pages + [
            per_b((1, rows, dh)), per_b((1, rows, dh))],
        out_specs=per_b((1, rows, dh)),
        scratch_shapes=[pltpu.VMEM((rows, 128), F32), pltpu.VMEM((rows, 1), F32), pltpu.VMEM((rows, 1), F32),
                        pltpu.VMEM((rows, dh), F32)],
    )
    return pl.pallas_call(
        kern,
        grid_spec=grid_spec,
        out_shape=jax.ShapeDtypeStruct((db, rows, dh), F32),
        compiler_params=_cparams(("parallel", "arbitrary")),
        name="moba_attn_sample",
    )(page_table, q, gs, *([cache_k] * pg), *([cache_v] * pg), k_new, v_new)


def _extract_topk(cur_ref, out_ref, k):
    groups = cur_ref.shape[0]

    def body(j, carry):
        cur = [cur_ref[g] for g in range(groups)]
        mx = [jnp.max(cur[g], axis=0, keepdims=True) for g in range(groups)]
        for g in range(groups):
            out_ref[g, pl.ds(j, 1), :] = mx[g]
            cur_ref[g] = jnp.where(cur[g] == mx[g], -jnp.inf, cur[g])
        return carry

    lax.fori_loop(0, k, body, 0)


_CAND_ROWS = PEER_TOPK + 7 * 8 + 8


def _peer_route_kernel(q_ref, sk_ref, s1m_ref, s2m_ref, eb_ref, a0_ref, thr_ref,
                       cur_ref, top_ref, cand_ref, best_ref):
    half = PEER_KEY_DIM // 2
    k = PEER_TOPK
    for h in range(PEER_HEADS):
        for c, raw_ref in enumerate((s1m_ref, s2m_ref)):
            qc = q_ref[:, (h * 2 + c) * half:(h * 2 + c + 1) * half]
            s = lax.dot_general(sk_ref[h, c], qc, _NT, preferred_element_type=F32,
                                precision=lax.Precision.HIGHEST)
            raw_ref[h] = s
            cur_ref[h * 2 + c] = s
    _extract_topk(cur_ref, top_ref, k)
    for h in range(PEER_HEADS):
        ta = top_ref[2 * h]
        tb = top_ref[2 * h + 1]
        cand_ref[h, 0:k, :] = ta[0:1, :] + tb
        for p in range(1, 8):
            cand_ref[h, k + (p - 1) * 8:k + p * 8, :] = ta[p:p + 1, :] + tb[0:8, :]
        cand_ref[h, k + 56:k + 64, :] = ta[8:16, :] + tb[0:1, :]
    _extract_topk(cand_ref, best_ref, k)
    for h in range(PEER_HEADS):
        ta = top_ref[2 * h]
        tb = top_ref[2 * h + 1]
        best = best_ref[h]
        z = jnp.sum(jnp.exp(best - best[0:1, :]), axis=0, keepdims=True)
        s1 = s1m_ref[h]
        s2 = s2m_ref[h]
        s1m_ref[h] = jnp.where(s1 >= ta[k - 1:k, :], s1, NEG_INF)
        s2m_ref[h] = jnp.where(s2 >= tb[k - 1:k, :], s2, NEG_INF)
        eb_ref[h] = jnp.exp(s2 - tb[0:1, :]) / z
        a0_ref[h:h + 1, :] = ta[0:1, :]
        thr_ref[h:h + 1, :] = best[k - 1:k, :]


def peer_route(q, sub_keys, tm=256):
    m = q.shape[0]
    tm = min(tm, m)
    nk = PEER_N_KEYS
    big = pl.BlockSpec((PEER_HEADS, nk, tm), lambda i: (0, 0, i))
    small = pl.BlockSpec((PEER_HEADS, tm), lambda i: (0, i))
    big_shape = jax.ShapeDtypeStruct((PEER_HEADS, nk, m), F32)
    small_shape = jax.ShapeDtypeStruct((PEER_HEADS, m), F32)
    return pl.pallas_call(
        _peer_route_kernel,
        grid=(m // tm,),
        in_specs=[pl.BlockSpec((tm, PEER_HEADS * PEER_KEY_DIM), lambda i: (i, 0)),
                  pl.BlockSpec(sub_keys.shape, lambda i: (0, 0, 0, 0))],
        out_specs=[big, big, big, small, small],
        out_shape=[big_shape, big_shape, big_shape, small_shape, small_shape],
        scratch_shapes=[pltpu.VMEM((2 * PEER_HEADS, nk, tm), F32),
                        pltpu.VMEM((2 * PEER_HEADS, PEER_TOPK, tm), F32),
                        pltpu.VMEM((PEER_HEADS, _CAND_ROWS, tm), F32),
                        pltpu.VMEM((PEER_HEADS, PEER_TOPK, tm), F32)],
        compiler_params=_cparams(("parallel",)),
        name="peer_route",
    )(q, sub_keys)


def _peer_expert_kernel(x_ref, u_ref, v_ref, s1m_ref, s2m_ref, eb_ref, a0_ref, thr_ref, o_ref, a_ref, *, te):
    e = pl.program_id(1)
    nk = PEER_N_KEYS

    @pl.when(e == 0)
    def _():
        o_ref[...] = jnp.zeros_like(o_ref)

    g = lax.dot_general(u_ref[...], x_ref[...], _NT, preferred_element_type=F32)
    for ii in range(te // nk):
        i = e * (te // nk) + ii
        w = jnp.zeros((nk, g.shape[1]), F32)
        for h in range(PEER_HEADS):
            s1 = s1m_ref[h, pl.ds(i, 1), :]
            ea = jnp.exp(s1 - a0_ref[h:h + 1, :])
            hit = (s1 + s2m_ref[h]) >= thr_ref[h:h + 1, :]
            w = w + jnp.where(hit, ea * eb_ref[h], 0.0)
        gi = g[ii * nk:(ii + 1) * nk, :]
        act = 0.5 * gi * (1.0 + lax.erf(gi * (2.0 ** -0.5)))
        a_ref[ii * nk:(ii + 1) * nk, :] = (w * act).astype(BF16)
    o_ref[...] += lax.dot_general(a_ref[...], v_ref[...], _TN, preferred_element_type=F32)


def peer_experts(x, u, v, layer, s1m, s2m, eb, a0, thr, tm=512, te=512):
    m, d = x.shape
    tm = min(tm, m)
    n_e = u.shape[1]
    kern = functools.partial(_peer_expert_kernel, te=te)
    once = pl.Buffered(1)
    big = pl.BlockSpec((PEER_HEADS, PEER_N_KEYS, tm), lambda i, e: (0, 0, i), pipeline_mode=once)
    small = pl.BlockSpec((PEER_HEADS, tm), lambda i, e: (0, i), pipeline_mode=once)
    return pl.pallas_call(
        kern,
        grid=(m // tm, n_e // te),
        in_specs=[pl.BlockSpec((tm, d), lambda i, e: (i, 0), pipeline_mode=once),
                  pl.BlockSpec((None, te, d), lambda i, e: (layer, e, 0)),
                  pl.BlockSpec((None, te, d), lambda i, e: (layer, e, 0)),
                  big, big, big, small, small],
        out_specs=pl.BlockSpec((tm, d), lambda i, e: (i, 0)),
        out_shape=jax.ShapeDtypeStruct((m, d), F32),
        scratch_shapes=[pltpu.VMEM((te, tm), BF16)],
        compiler_params=_cparams(("parallel", "arbitrary")),
        name="peer_experts",
    )(x, u, v, s1m, s2m, eb, a0, thr)


def _token_stage(x, x_bf, o_a, o_b, o_c, wts):
    merged = gated_merge(x_bf, o_a, o_b, o_c, wts['w_gate'], wts['b_gate'], wts['w_br_a'], wts['w_br_b'],
                         wts['w_br_c'], wts['layer'])
    h, h_bf = matmul_residual_ln(merged, wts['w_o'], wts['layer'], x, wts['ln1_g'], wts['ln1_b'])
    q = matmul(h_bf, wts['peer_wq'], wts['layer'])
    s1m, s2m, eb, a0, thr = peer_route(q, wts['peer_subkeys'])
    f = peer_experts(h_bf, wts['peer_u'], wts['peer_v'], wts['layer'], s1m, s2m, eb, a0, thr)
    return add_ln(h, f, wts['ln2_g'], wts['ln2_b'])


def kernel(x_prompt, x_sample, cache_diff_k, cache_diff_v, cache_moba_k, cache_moba_v, state_conv, page_table,
           w_in, da_lambda, da_subln_g, conv_w, w_gate, b_gate, w_br_a, w_br_b, w_br_c, w_o, ln1_g, ln1_b,
           peer_wq, peer_subkeys, peer_u, peer_v, ln2_g, ln2_b):
    batch, seq, d = x_prompt.shape
    db, t_len, _ = x_sample.shape
    n_p = batch * seq
    n_s = db * t_len
    pad_s = 128
    ck_a, cv_a = split_component_view(cache_diff_k), split_component_view(cache_diff_v)
    ck_b, cv_b = cache_moba_k, cache_moba_v

    xp = x_prompt.reshape(n_p, d)
    xs = jnp.pad(x_sample.reshape(n_s, d), ((0, pad_s - n_s), (0, 0)))
    xp_bf = xp.astype(BF16)
    xs_bf = xs.astype(BF16)
    w_o_bf, peer_u_bf, peer_v_bf = w_o.astype(BF16), peer_u.astype(BF16), peer_v.astype(BF16)

    st_p = [[] for _ in range(5)]
    st_s = [[] for _ in range(5)]
    for l in range(DEPTH):
        lam_init = 0.8 - 0.6 * math.exp(-0.3 * l)
        wts = {
            'layer': l,
            'w_gate': w_gate, 'b_gate': b_gate.reshape(DEPTH, 1, -1),
            'w_br_a': w_br_a, 'w_br_b': w_br_b, 'w_br_c': w_br_c,
            'w_o': w_o_bf, 'ln1_g': ln1_g[l].reshape(1, -1), 'ln1_b': ln1_b[l].reshape(1, -1),
            'peer_wq': peer_wq, 'peer_subkeys': peer_subkeys[l],
            'peer_u': peer_u_bf, 'peer_v': peer_v_bf,
            'ln2_g': ln2_g[l].reshape(1, -1), 'ln2_b': ln2_b[l].reshape(1, -1),
        }
        lam_p = da_lambda[l]
        sub_g = da_subln_g[l].reshape(1, -1)

        u_p = matmul(xp_bf, w_in, l)
        o_a = diff_attn_prompt(u_p, lam_p, sub_g, batch, seq, lam_init)
        o_b = moba_prompt(u_p, batch, seq)
        o_c, tail = conv_prompt(u_p, conv_w[l], batch, seq)
        xp, xp_bf = _token_stage(xp, xp_bf, o_a, o_b, o_c, wts)
        u3 = u_p.reshape(batch, seq, IN_WIDTH)
        st_p[0].append(u3[:, :, OFF_KA:OFF_KA + DA_K].reshape(batch, seq, DA_KV_HEADS, 2 * DA_HEAD_DIM))
        st_p[1].append(u3[:, :, OFF_VA:OFF_VA + DA_V].reshape(batch, seq, DA_KV_HEADS, 2 * DA_HEAD_DIM))
        st_p[2].append(u3[:, :, OFF_KB:OFF_KB + MB_W].reshape(batch, seq, MB_HEADS, MB_HEAD_DIM))
        st_p[3].append(u3[:, :, OFF_VB:OFF_VB + MB_W].reshape(batch, seq, MB_HEADS, MB_HEAD_DIM))
        st_p[4].append(tail[:, 8 - (CONV_W - 1):, :])

        u_s = matmul(xs_bf, w_in, l)[:n_s].reshape(db, t_len, IN_WIDTH)
        seg = lambda off, width: u_s[:, :, off:off + width]
        q_a = seg(OFF_QA, DA_Q).reshape(db, t_len, DA_KV_HEADS, 2, 2, DA_HEAD_DIM)
        q_a = q_a.transpose(0, 2, 4, 3, 1, 5).reshape(db, 4 * DA_KV_HEADS * t_len, DA_HEAD_DIM)
        k_a, v_a = seg(OFF_KA, DA_K), seg(OFF_VA, DA_V)
        new_rows = lambda a: split_component_view(a.reshape(db, t_len, DA_KV_HEADS, 2 * DA_HEAD_DIM)).reshape(
            db, t_len * 2 * DA_KV_HEADS, DA_HEAD_DIM)
        oa_s = diff_attn_sample(q_a, ck_a, cv_a, new_rows(k_a), new_rows(v_a), page_table, lam_p, sub_g,
                                l, lam_init, t_len)
        oa_s = oa_s.reshape(db, DA_KV_HEADS, 2, t_len, 2 * DA_HEAD_DIM).transpose(0, 3, 1, 2, 4)
        oa_s = oa_s.reshape(n_s, DA_Q)

        q_b = seg(OFF_QB, MB_W).reshape(db, t_len, MB_HEADS, MB_HEAD_DIM)
        k_b, v_b = seg(OFF_KB, MB_W), seg(OFF_VB, MB_W)
        head_major = lambda a: a.transpose(0, 2, 1, 3).reshape(db, MB_HEADS * t_len, a.shape[-1])
        token_head = lambda a: a.reshape(db, t_len * MB_HEADS, MB_HEAD_DIM)
        gs = moba_gate_sample(q_b, ck_b, page_table, l)
        ob_s = moba_attn_sample(head_major(q_b), head_major(gs), ck_b, cv_b, token_head(k_b), token_head(v_b),
                                page_table, l)
        ob_s = ob_s.reshape(db, MB_HEADS, t_len, MB_HEAD_DIM).transpose(0, 2, 1, 3).reshape(n_s, MB_W)

        tmaj = lambda off: seg(off, CONV_DIM).transpose(1, 0, 2)
        oc_s, nst = conv_sample(tmaj(OFF_XC), tmaj(OFF_GB), tmaj(OFF_GC), state_conv[l].transpose(1, 0, 2),
                                conv_w[l])
        oc_s = oc_s.transpose(1, 0, 2).reshape(n_s, CONV_DIM)

        pad_tok = lambda a: jnp.pad(a, ((0, pad_s - n_s), (0, 0))).astype(BF16)
        xs, xs_bf = _token_stage(xs, xs_bf, pad_tok(oa_s), pad_tok(ob_s), pad_tok(oc_s), wts)
        st_s[0].append(k_a.reshape(db, t_len, DA_KV_HEADS, 2 * DA_HEAD_DIM))
        st_s[1].append(v_a.reshape(db, t_len, DA_KV_HEADS, 2 * DA_HEAD_DIM))
        st_s[2].append(k_b.reshape(db, t_len, MB_HEADS, MB_HEAD_DIM))
        st_s[3].append(v_b.reshape(db, t_len, MB_HEADS, MB_HEAD_DIM))
        st_s[4].append(nst.transpose(1, 0, 2))

    y_p = xp.reshape(batch, seq, d)
    y_s = xs[:n_s].reshape(db, t_len, d)
    stack = lambda parts: jnp.stack(parts, axis=0)
    return (y_p, y_s, stack(st_p[0]), stack(st_p[1]), stack(st_p[2]), stack(st_p[3]), stack(st_p[4]),
            stack(st_s[0]), stack(st_s[1]), stack(st_s[2]), stack(st_s[3]), stack(st_s[4]))
```

```python
import functools
import math

import jax
import jax.numpy as jnp
from jax import lax
from jax.experimental import pallas as pl
from jax.experimental.pallas import tpu as pltpu

F32 = jnp.float32
BF16 = jnp.bfloat16

D_MODEL = 4096
DEPTH = 2
PAGE_SIZE = 128
DA_HEADS = 8
DA_KV_HEADS = 4
DA_HEAD_DIM = 128
DA_Q = DA_HEADS * 2 * DA_HEAD_DIM
DA_K = DA_KV_HEADS * 2 * DA_HEAD_DIM
DA_V = DA_K
MB_HEADS = 8
MB_HEAD_DIM = 128
MB_W = MB_HEADS * MB_HEAD_DIM
MB_BLOCK = 256
MB_TOPK = 3
CONV_DIM = 1024
CONV_W = 3
IN_WIDTH = DA_Q + DA_K + DA_V + 3 * MB_W + 3 * CONV_DIM
PEER_HEADS = 8
PEER_N_KEYS = 128
PEER_N_EXPERTS = PEER_N_KEYS * PEER_N_KEYS
PEER_KEY_DIM = 256
PEER_TOPK = 16
ALPHA = (2.0 * DEPTH) ** 0.25
LN_EPS = 1e-5
NEG_INF = -1e30

OFF_QA = 0
OFF_KA = DA_Q
OFF_VA = OFF_KA + DA_K
OFF_QB = OFF_VA + DA_V
OFF_KB = OFF_QB + MB_W
OFF_VB = OFF_KB + MB_W
OFF_XC = OFF_VB + MB_W
OFF_GB = OFF_XC + CONV_DIM
OFF_GC = OFF_GB + CONV_DIM

VMEM_LIMIT = 56 * 1024 * 1024

_NT = (((1,), (1,)), ((), ()))
_TN = (((0,), (0,)), ((), ()))


def _cparams(sem):
    return pltpu.CompilerParams(dimension_semantics=sem, vmem_limit_bytes=VMEM_LIMIT)


_ONCE = pl.Buffered(1)


def _mm_kernel(x_ref, w_ref, o_ref, wbf_ref):
    @pl.when(pl.program_id(1) == 0)
    def _():
        wbf_ref[...] = w_ref[...].astype(BF16)

    o_ref[...] = jnp.dot(x_ref[...], wbf_ref[...], preferred_element_type=F32).astype(o_ref.dtype)


def matmul(x, w, layer, out_dtype=F32, tm=1024, tn=512):
    m, k = x.shape
    n = w.shape[2]
    tm = min(tm, m)
    tn = min(tn, n)
    return pl.pallas_call(
        _mm_kernel,
        grid=(n // tn, m // tm),
        in_specs=[pl.BlockSpec((tm, k), lambda j, i: (i, 0)),
                  pl.BlockSpec((None, k, tn), lambda j, i: (layer, 0, j))],
        out_specs=pl.BlockSpec((tm, tn), lambda j, i: (i, j)),
        out_shape=jax.ShapeDtypeStruct((m, n), out_dtype),
        scratch_shapes=[pltpu.VMEM((k, tn), BF16)],
        compiler_params=_cparams(("parallel", "arbitrary")),
        name="matmul",
    )(x, w)


def _merge_kernel(x_ref, oa_ref, ob_ref, oc_ref, wga_ref, wgb_ref, wgc_ref,
                  bga_ref, bgb_ref, bgc_ref, wa_ref, wb_ref, wc_ref, o_ref,
                  wg_bf, wa_bf, wb_bf, wc_bf):
    @pl.when(pl.program_id(1) == 0)
    def _():
        for b, w_r in enumerate((wga_ref, wgb_ref, wgc_ref)):
            wg_bf[b] = w_r[...].astype(BF16)
        wa_bf[...] = wa_ref[...].astype(BF16)
        wb_bf[...] = wb_ref[...].astype(BF16)
        wc_bf[...] = wc_ref[...].astype(BF16)

    x = x_ref[...]
    pre = [jnp.dot(x, wg_bf[b], preferred_element_type=F32) + bg_r[...]
           for b, bg_r in enumerate((bga_ref, bgb_ref, bgc_ref))]
    br = [jnp.dot(o_r[...], w_r[...], preferred_element_type=F32)
          for o_r, w_r in ((oa_ref, wa_bf), (ob_ref, wb_bf), (oc_ref, wc_bf))]
    gates = [jax.nn.sigmoid(z) for z in pre]
    merged = gates[0] * br[0] + gates[1] * br[1] + gates[2] * br[2]
    o_ref[...] = merged.astype(o_ref.dtype)


def gated_merge(x, o_a, o_b, o_c, w_gate, b_gate, w_a, w_b, w_c, layer, tm=512, tn=256):
    m, d = x.shape
    tm = min(tm, m)
    nj = d // tn
    row = lambda j, i: (i, 0)
    panel = lambda rows, off, mode=_ONCE: pl.BlockSpec((None, rows, tn), lambda j, i: (layer, 0, j + off),
                                                       pipeline_mode=mode)
    bias = lambda off: pl.BlockSpec((None, 1, tn), lambda j, i: (layer, 0, j + off))
    return pl.pallas_call(
        _merge_kernel,
        grid=(nj, m // tm),
        in_specs=[pl.BlockSpec((tm, d), row),
                  pl.BlockSpec((tm, o_a.shape[1]), row),
                  pl.BlockSpec((tm, o_b.shape[1]), row),
                  pl.BlockSpec((tm, o_c.shape[1]), row),
                  panel(d, 0), panel(d, nj), panel(d, 2 * nj),
                  bias(0), bias(nj), bias(2 * nj),
                  panel(w_a.shape[1], 0, None), panel(w_b.shape[1], 0, None), panel(w_c.shape[1], 0, None)],
        out_specs=pl.BlockSpec((tm, tn), lambda j, i: (i, j)),
        out_shape=jax.ShapeDtypeStruct((m, d), BF16),
        scratch_shapes=[pltpu.VMEM((3, d, tn), BF16), pltpu.VMEM((w_a.shape[1], tn), BF16),
                        pltpu.VMEM((w_b.shape[1], tn), BF16), pltpu.VMEM((w_c.shape[1], tn), BF16)],
        compiler_params=_cparams(("parallel", "arbitrary")),
        name="gated_merge",
    )(x, o_a, o_b, o_c, w_gate, w_gate, w_gate, b_gate, b_gate, b_gate, w_a, w_b, w_c)


def _layer_norm(z, g, b):
    mu = jnp.mean(z, axis=-1, keepdims=True)
    zc = z - mu
    var = jnp.mean(zc * zc, axis=-1, keepdims=True)
    return zc * lax.rsqrt(var + LN_EPS) * g + b


def _mm_ln_kernel(a_ref, w_ref, res_ref, g_ref, b_ref, o_ref, obf_ref):
    k = pl.program_id(1)

    @pl.when(k == 0)
    def _():
        o_ref[...] = jnp.zeros_like(o_ref)

    o_ref[...] += jnp.dot(a_ref[...], w_ref[...], preferred_element_type=F32)

    @pl.when(k == pl.num_programs(1) - 1)
    def _():
        y = _layer_norm(ALPHA * res_ref[...] + o_ref[...], g_ref[...], b_ref[...])
        o_ref[...] = y
        obf_ref[...] = y.astype(BF16)


def matmul_residual_ln(a, w, layer, res, g, b, tm=512, tk=512):
    m, kdim = a.shape
    d = w.shape[2]
    tm = min(tm, m)
    return pl.pallas_call(
        _mm_ln_kernel,
        grid=(m // tm, kdim // tk),
        in_specs=[pl.BlockSpec((tm, tk), lambda i, k: (i, k)),
                  pl.BlockSpec((None, tk, d), lambda i, k: (layer, k, 0)),
                  pl.BlockSpec((tm, d), lambda i, k: (i, 0), pipeline_mode=pl.Buffered(1)),
                  pl.BlockSpec((1, d), lambda i, k: (0, 0)),
                  pl.BlockSpec((1, d), lambda i, k: (0, 0))],
        out_specs=[pl.BlockSpec((tm, d), lambda i, k: (i, 0)),
                   pl.BlockSpec((tm, d), lambda i, k: (i, 0))],
        out_shape=[jax.ShapeDtypeStruct((m, d), F32), jax.ShapeDtypeStruct((m, d), BF16)],
        compiler_params=_cparams(("parallel", "arbitrary")),
        name="matmul_residual_ln",
    )(a, w, res, g, b)


def _add_ln_kernel(res_ref, f_ref, g_ref, b_ref, o_ref, obf_ref):
    y = _layer_norm(ALPHA * res_ref[...] + f_ref[...], g_ref[...], b_ref[...])
    o_ref[...] = y
    obf_ref[...] = y.astype(BF16)


def add_ln(res, f, g, b, tm=256):
    m, d = res.shape
    tm = min(tm, m)
    spec = pl.BlockSpec((tm, d), lambda i: (i, 0))
    vec = pl.BlockSpec((1, d), lambda i: (0, 0))
    return pl.pallas_call(
        _add_ln_kernel,
        grid=(m // tm,),
        in_specs=[spec, spec, vec, vec],
        out_specs=[spec, spec],
        out_shape=[jax.ShapeDtypeStruct((m, d), F32), jax.ShapeDtypeStruct((m, d), BF16)],
        compiler_params=_cparams(("parallel",)),
        name="add_ln",
    )(res, f, g, b)


def _lambda_value(lp, lam_init):
    a = jnp.sum(lp[0:1, :] * lp[1:2, :], axis=-1, keepdims=True)
    b = jnp.sum(lp[2:3, :] * lp[3:4, :], axis=-1, keepdims=True)
    return jnp.exp(a) - jnp.exp(b) + lam_init


def _online_update(s, v_bf, m_ref, l_ref, acc_ref):
    m_prev = m_ref[...]
    m_new = jnp.maximum(m_prev, jnp.max(s, axis=-1, keepdims=True))
    alpha = jnp.exp(m_prev - m_new)
    p = jnp.exp(s - m_new)
    l_ref[...] = alpha * l_ref[...] + jnp.sum(p, axis=-1, keepdims=True)
    acc_ref[...] = alpha * acc_ref[...] + jnp.dot(p.astype(BF16), v_bf, preferred_element_type=F32)
    m_ref[...] = m_new


def _subln(o, g, lam_init):
    return o * lax.rsqrt(jnp.mean(o * o, axis=-1, keepdims=True) + LN_EPS) * g * (1.0 - lam_init)


def _online_update_keymajor(scores, vts, m_ref, l_ref, acc_ref):
    chains = range(len(scores))
    m_prev = [m_ref[c] for c in chains]
    m_new = [jnp.maximum(m_prev[c], jnp.max(scores[c], axis=0, keepdims=True)) for c in chains]
    alpha = [jnp.exp(m_prev[c] - m_new[c]) for c in chains]
    p = [jnp.exp(scores[c] - m_new[c]) for c in chains]
    for c in chains:
        l_ref[c] = alpha[c] * l_ref[c] + jnp.sum(p[c], axis=0, keepdims=True)
        m_ref[c] = m_new[c]
    pv = [jnp.dot(vts[c], p[c].astype(BF16), preferred_element_type=F32) for c in chains]
    for c in chains:
        acc_ref[c] = alpha[c] * acc_ref[c] + pv[c]


def _diff_prompt_kernel(q_ref, k_ref, v_ref, lam_ref, g_ref, o_ref, vt_ref, m_ref, l_ref, acc_ref, *,
                        lam_init, tq):
    i = pl.program_id(2)
    dh = DA_HEAD_DIM
    scale = dh ** -0.5
    n_kv = v_ref.shape[0] // tq

    @pl.when(i == 0)
    def _():
        for j in range(n_kv):
            vt_ref[j] = v_ref[j * tq:(j + 1) * tq, :].T.astype(BF16)

    lam = _lambda_value(lam_ref[...], lam_init)
    key = lax.broadcasted_iota(jnp.int32, (tq, 2 * tq), 0)
    qry = lax.broadcasted_iota(jnp.int32, (tq, 2 * tq), 1)
    causal = key <= jnp.where(qry >= tq, qry - tq, qry)
    qs = [(jnp.concatenate([q_ref[:, c * dh:(c + 1) * dh], q_ref[:, (2 + c) * dh:(3 + c) * dh]], axis=0)
           * scale).astype(BF16) for c in range(2)]
    m_ref[...] = jnp.full_like(m_ref, NEG_INF)
    l_ref[...] = jnp.zeros_like(l_ref)
    acc_ref[...] = jnp.zeros_like(acc_ref)

    def tile(j, masked):
        start = pl.multiple_of(j * tq, tq)
        vt = vt_ref[j]
        scores = []
        for c in range(2):
            kt = k_ref[pl.ds(start, tq), c * dh:(c + 1) * dh].astype(BF16)
            s = lax.dot_general(kt, qs[c], _NT, preferred_element_type=F32)
            scores.append(jnp.where(causal, s, NEG_INF) if masked else s)
        _online_update_keymajor(scores, [vt, vt], m_ref, l_ref, acc_ref)

    def body(j, carry):
        tile(j, False)
        return carry

    lax.fori_loop(0, i, body, 0)
    tile(i, True)
    ot = acc_ref[0] / l_ref[0] - lam * (acc_ref[1] / l_ref[1])
    ot = ot * lax.rsqrt(jnp.mean(ot * ot, axis=0, keepdims=True) + LN_EPS) * g_ref[...] * (1.0 - lam_init)
    o = ot.T.astype(o_ref.dtype)
    o_ref[:, 0:2 * dh] = o[:tq]
    o_ref[:, 2 * dh:4 * dh] = o[tq:]


def diff_attn_prompt(u, lam_p, subln_g, batch, seq, lam_init, tq=512):
    nq = seq // tq
    qw = 4 * DA_HEAD_DIM
    kw = 2 * DA_HEAD_DIM
    kern = functools.partial(_diff_prompt_kernel, lam_init=lam_init, tq=tq)
    return pl.pallas_call(
        kern,
        grid=(batch, DA_KV_HEADS, nq),
        in_specs=[pl.BlockSpec((tq, qw), lambda b, k, i: (b * nq + i, k)),
                  pl.BlockSpec((seq, kw), lambda b, k, i: (b, OFF_KA // kw + k)),
                  pl.BlockSpec((seq, kw), lambda b, k, i: (b, OFF_VA // kw + k)),
                  pl.BlockSpec((4, DA_HEAD_DIM), lambda b, k, i: (0, 0)),
                  pl.BlockSpec((kw, 1), lambda b, k, i: (0, 0))],
        out_specs=pl.BlockSpec((tq, qw), lambda b, k, i: (b * nq + i, k)),
        out_shape=jax.ShapeDtypeStruct((batch * seq, DA_Q), BF16),
        scratch_shapes=[pltpu.VMEM((nq, kw, tq), BF16),
                        pltpu.VMEM((2, 1, 2 * tq), F32), pltpu.VMEM((2, 1, 2 * tq), F32),
                        pltpu.VMEM((2, kw, 2 * tq), F32)],
        compiler_params=_cparams(("parallel", "parallel", "arbitrary")),
        name="diff_attn_prompt",
    )(u, u, u, lam_p, subln_g.reshape(kw, 1))


def _moba_prompt_kernel(q_ref, k_ref, v_ref, o_ref, km_ref, vt_ref, sel_ref, m_ref, l_ref, acc_ref, *, nb, hg):
    i = pl.program_id(2)
    blk = MB_BLOCK
    dh = MB_HEAD_DIM
    scale = dh ** -0.5

    @pl.when(i == 0)
    def _():
        km_ref[...] = jnp.zeros_like(km_ref)
        for n in range(nb):
            km_ref[n:n + 1, :] = jnp.mean(k_ref[n * blk:(n + 1) * blk, :], axis=0, keepdims=True)
            for h in range(hg):
                vt_ref[h, n] = v_ref[n * blk:(n + 1) * blk, h * dh:(h + 1) * dh].T.astype(BF16)

    heads = range(hg)
    qf = [q_ref[:, h * dh:(h + 1) * dh] for h in heads]
    qs = [(qf[h] * scale).astype(BF16) for h in heads]
    gs = [lax.dot_general(km_ref[:, h * dh:(h + 1) * dh], qf[h], _NT, preferred_element_type=F32,
                          precision=lax.Precision.HIGHEST) for h in heads]
    block = lax.broadcasted_iota(jnp.int32, gs[0].shape, 0)
    past = block < i
    for n in range(nb):
        ahead = [(gs[h] > gs[h][n:n + 1, :]) | ((gs[h] == gs[h][n:n + 1, :]) & (block < n)) for h in heads]
        rank = [jnp.sum(jnp.where(past & ahead[h], 1.0, 0.0), axis=0, keepdims=True) for h in heads]
        for h in heads:
            sel_ref[h, n] = jnp.where(rank[h] < MB_TOPK, 1.0, 0.0)

    key = lax.broadcasted_iota(jnp.int32, (blk, blk), 0)
    qry = lax.broadcasted_iota(jnp.int32, (blk, blk), 1)
    m_ref[...] = jnp.full_like(m_ref, NEG_INF)
    l_ref[...] = jnp.zeros_like(l_ref)
    acc_ref[...] = jnp.zeros_like(acc_ref)

    def tile(n, mask_of):
        start = pl.multiple_of(n * blk, blk)
        scores = []
        for h in range(hg):
            kt = k_ref[pl.ds(start, blk), h * dh:(h + 1) * dh].astype(BF16)
            s = lax.dot_general(kt, qs[h], _NT, preferred_element_type=F32)
            scores.append(jnp.where(mask_of(h), s, NEG_INF))
        _online_update_keymajor(scores, [vt_ref[h, n] for h in range(hg)], m_ref, l_ref, acc_ref)

    tile(i, lambda h: key <= qry)

    def body(n, carry):
        tile(n, lambda h: sel_ref[h, n] > 0.5)
        return carry

    lax.fori_loop(0, i, body, 0)
    for h in range(hg):
        o_ref[:, h * dh:(h + 1) * dh] = (acc_ref[h] / l_ref[h]).T.astype(o_ref.dtype)


def moba_prompt(u, batch, seq, hg=MB_HEADS):
    nb = seq // MB_BLOCK
    dh = MB_HEAD_DIM
    w = hg * dh
    kern = functools.partial(_moba_prompt_kernel, nb=nb, hg=hg)
    once = pl.Buffered(1)
    return pl.pallas_call(
        kern,
        grid=(batch, MB_HEADS // hg, nb),
        in_specs=[pl.BlockSpec((MB_BLOCK, w), lambda b, h, i: (b * nb + i, OFF_QB // w + h)),
                  pl.BlockSpec((seq, w), lambda b, h, i: (b, OFF_KB // w + h), pipeline_mode=once),
                  pl.BlockSpec((seq, w), lambda b, h, i: (b, OFF_VB // w + h), pipeline_mode=once)],
        out_specs=pl.BlockSpec((MB_BLOCK, w), lambda b, h, i: (b * nb + i, h)),
        out_shape=jax.ShapeDtypeStruct((batch * seq, MB_W), BF16),
        scratch_shapes=[pltpu.VMEM((-(-nb // 8) * 8, w), F32), pltpu.VMEM((hg, nb, dh, MB_BLOCK), BF16),
                        pltpu.VMEM((hg, nb, 1, MB_BLOCK), F32),
                        pltpu.VMEM((hg, 1, MB_BLOCK), F32), pltpu.VMEM((hg, 1, MB_BLOCK), F32),
                        pltpu.VMEM((hg, dh, MB_BLOCK), F32)],
        compiler_params=_cparams(("parallel", "parallel", "arbitrary")),
        name="moba_prompt",
    )(u, u, u)


def _conv_prompt_kernel(xc_ref, gb_ref, gc_ref, w_ref, o_ref, st_ref):
    z = gc_ref[...] * xc_ref[...]
    t = z.shape[0]
    row = lax.broadcasted_iota(jnp.int32, z.shape, 0)
    z1 = jnp.where(row >= 1, pltpu.roll(z, 1, 0), 0.0)
    z2 = jnp.where(row >= 2, pltpu.roll(z, 2, 0), 0.0)
    y = w_ref[0:1, :] * z2 + w_ref[1:2, :] * z1 + w_ref[2:3, :] * z
    o_ref[...] = (gb_ref[...] * y).astype(o_ref.dtype)
    st_ref[0] = z[t - 8:, :]


def conv_prompt(u, conv_w, batch, seq, tc=256):
    nc = CONV_DIM // tc
    blk = lambda off: pl.BlockSpec((seq, tc), lambda b, c: (b, off // tc + c))
    return pl.pallas_call(
        _conv_prompt_kernel,
        grid=(batch, nc),
        in_specs=[blk(OFF_XC), blk(OFF_GB), blk(OFF_GC), pl.BlockSpec((CONV_W, tc), lambda b, c: (0, c))],
        out_specs=[pl.BlockSpec((seq, tc), lambda b, c: (b, c)),
                   pl.BlockSpec((1, 8, tc), lambda b, c: (b, 0, c))],
        out_shape=[jax.ShapeDtypeStruct((batch * seq, CONV_DIM), BF16),
                   jax.ShapeDtypeStruct((batch, 8, CONV_DIM), F32)],
        compiler_params=_cparams(("parallel", "parallel")),
        name="conv_prompt",
    )(u, u, u, conv_w)


def _conv_sample_kernel(xc_ref, gb_ref, gc_ref, st_ref, w_ref, o_ref, nst_ref):
    t_len = xc_ref.shape[0]
    zz = [st_ref[0], st_ref[1]] + [gc_ref[t] * xc_ref[t] for t in range(t_len)]
    for t in range(t_len):
        y = w_ref[0:1, :] * zz[t] + w_ref[1:2, :] * zz[t + 1] + w_ref[2:3, :] * zz[t + 2]
        o_ref[t] = gb_ref[t] * y
    nst_ref[0] = zz[t_len]
    nst_ref[1] = zz[t_len + 1]


def conv_sample(xc, gb, gc, state, conv_w):
    return pl.pallas_call(
        _conv_sample_kernel,
        out_shape=[jax.ShapeDtypeStruct(xc.shape, F32), jax.ShapeDtypeStruct(state.shape, F32)],
        name="conv_sample",
    )(xc, gb, gc, state, conv_w)


def _page_specs(layer, pg, heads, dh):
    return [pl.BlockSpec((None, None, PAGE_SIZE, heads, dh),
                         lambda b, p, pt, j=j: (layer, pt[b, p * pg + j], 0, 0, 0)) for j in range(pg)]


def _token_head_rows(page_refs, dtype=BF16):
    ps, heads, dh = page_refs[0].shape
    return jnp.concatenate([r[...].reshape(ps * heads, dh).astype(dtype) for r in page_refs], axis=0)


def split_component_view(x):
    lead = x.shape[:-2]
    kv, w = x.shape[-2:]
    n = len(lead)
    x = x.reshape(*lead, kv, 2, w // 2)
    return x.transpose(*range(n), n + 1, n, n + 2).reshape(*lead, 2 * kv, w // 2)


def _swap_row_halves(x):
    n, d = x.shape
    return pltpu.roll(x.reshape(n // 8, 8, d), 4, 1).reshape(n, d)


def _diff_sample_kernel(pt_ref, q_ref, *refs, lam_init, t_len, pg):
    k_refs, v_refs = refs[:pg], refs[pg:2 * pg]
    kn_ref, vn_ref, lam_ref, g_ref, o_ref, m_ref, l_ref, acc_ref = refs[2 * pg:]
    p = pl.program_id(1)
    dh = DA_HEAD_DIM
    kv = DA_KV_HEADS
    scale = dh ** -0.5
    per_comp = 2 * t_len
    rows = kv * 2 * per_comp
    q = q_ref[0].astype(BF16)

    def own_rows(n_keys):
        r = lax.broadcasted_iota(jnp.int32, (rows, n_keys), 0)
        j = lax.broadcasted_iota(jnp.int32, (rows, n_keys), 1)
        target = ((r // per_comp) % 2) * kv + r // (2 * per_comp)
        return (j % (2 * kv)) == target, r, j

    def values(v_f32):
        return jnp.concatenate([v_f32, _swap_row_halves(v_f32)], axis=-1).astype(BF16)

    @pl.when(p == 0)
    def _():
        m_ref[...] = jnp.full_like(m_ref, NEG_INF)
        l_ref[...] = jnp.zeros_like(l_ref)
        acc_ref[...] = jnp.zeros_like(acc_ref)

    s = lax.dot_general(q, _token_head_rows(k_refs), _NT, preferred_element_type=F32) * scale
    own, _, _ = own_rows(s.shape[1])
    s = jnp.where(own, s, NEG_INF)
    _online_update(s, values(_token_head_rows(v_refs, F32)), m_ref, l_ref, acc_ref)

    @pl.when(p == pl.num_programs(1) - 1)
    def _():
        sn = lax.dot_general(q, kn_ref[0].astype(BF16), _NT, preferred_element_type=F32) * scale
        own_n, r, j = own_rows(sn.shape[1])
        sn = jnp.where(own_n & ((j // (2 * kv)) <= (r % t_len)), sn, NEG_INF)
        _online_update(sn, values(vn_ref[0]), m_ref, l_ref, acc_ref)
        o = acc_ref[...] / l_ref[...]
        lam = _lambda_value(lam_ref[...], lam_init)
        for k in range(kv):
            base = k * 2 * per_comp
            o1 = o[base:base + per_comp, :]
            o2 = o[base + per_comp:base + 2 * per_comp, :]
            o2 = jnp.concatenate([o2[:, dh:], o2[:, :dh]], axis=-1)
            o_ref[0, k] = _subln(o1 - lam * o2, g_ref[...], lam_init)


def diff_attn_sample(q, cache_k, cache_v, k_new, v_new, page_table, lam_p, subln_g, layer, lam_init, t_len,
                     pg=8):
    db, n_pages = page_table.shape
    rows, dh = q.shape[1:]
    per_comp = 2 * t_len
    kern = functools.partial(_diff_sample_kernel, lam_init=lam_init, t_len=t_len, pg=pg)
    pages = _page_specs(layer, pg, 2 * DA_KV_HEADS, dh)
    per_b = lambda shape: pl.BlockSpec(shape, lambda b, p, pt: (b, 0, 0))
    n_new = k_new.shape[1]
    grid_spec = pltpu.PrefetchScalarGridSpec(
        num_scalar_prefetch=1,
        grid=(db, n_pages // pg),
        in_specs=[per_b((1, rows, dh))] + pages + pages + [
            per_b((1, n_new, dh)), per_b((1, n_new, dh)),
            pl.BlockSpec((4, dh), lambda b, p, pt: (0, 0)),
            pl.BlockSpec((1, 2 * dh), lambda b, p, pt: (0, 0))],
        out_specs=pl.BlockSpec((1, DA_KV_HEADS, per_comp, 2 * dh), lambda b, p, pt: (b, 0, 0, 0)),
        scratch_shapes=[pltpu.VMEM((rows, 1), F32), pltpu.VMEM((rows, 1), F32), pltpu.VMEM((rows, 2 * dh), F32)],
    )
    return pl.pallas_call(
        kern,
        grid_spec=grid_spec,
        out_shape=jax.ShapeDtypeStruct((db, DA_KV_HEADS, per_comp, 2 * dh), F32),
        compiler_params=_cparams(("parallel", "arbitrary")),
        name="diff_attn_sample",
    )(page_table, q, *([cache_k] * pg), *([cache_v] * pg), k_new, v_new, lam_p, subln_g)


def _moba_gate_kernel(pt_ref, q_ref, *refs, pg, t_len):
    k_refs, gs_ref = refs[:pg], refs[pg]
    p = pl.program_id(1)
    pages_per_block = MB_BLOCK // PAGE_SIZE
    blocks = pg // pages_per_block

    @pl.when(p == 0)
    def _():
        gs_ref[...] = jnp.zeros_like(gs_ref)

    lane = lax.broadcasted_iota(jnp.int32, gs_ref.shape[2:], 1)
    for blk in range(blocks):
        page_sums = [jnp.sum(k_refs[blk * pages_per_block + j][...], axis=0) for j in range(pages_per_block)]
        km = functools.reduce(lambda a, b: a + b, page_sums) / MB_BLOCK
        n = p * blocks + blk
        for t in range(t_len):
            g = jnp.sum(q_ref[0, t] * km, axis=-1, keepdims=True)
            gs_ref[0, t] += jnp.where(lane == n, g, 0.0)


def moba_gate_sample(q, cache_k, page_table, layer, pg=8):
    db, n_pages = page_table.shape
    t_len = q.shape[1]
    kern = functools.partial(_moba_gate_kernel, pg=pg, t_len=t_len)
    grid_spec = pltpu.PrefetchScalarGridSpec(
        num_scalar_prefetch=1,
        grid=(db, n_pages // pg),
        in_specs=[pl.BlockSpec((1, t_len, MB_HEADS, MB_HEAD_DIM), lambda b, p, pt: (b, 0, 0, 0))]
        + _page_specs(layer, pg, MB_HEADS, MB_HEAD_DIM),
        out_specs=pl.BlockSpec((1, t_len, MB_HEADS, 128), lambda b, p, pt: (b, 0, 0, 0)),
    )
    return pl.pallas_call(
        kern,
        grid_spec=grid_spec,
        out_shape=jax.ShapeDtypeStruct((db, t_len, MB_HEADS, 128), F32),
        compiler_params=_cparams(("parallel", "arbitrary")),
        name="moba_gate_sample",
    )(page_table, q, *([cache_k] * pg))


def _moba_sample_kernel(pt_ref, q_ref, gs_ref, *refs, t_len, n_blocks, pg):
    k_refs, v_refs = refs[:pg], refs[pg:2 * pg]
    kn_ref, vn_ref, o_ref, sel_ref, m_ref, l_ref, acc_ref = refs[2 * pg:]
    p = pl.program_id(1)
    scale = MB_HEAD_DIM ** -0.5
    heads = MB_HEADS
    rows = heads * t_len
    block_keys = MB_BLOCK * heads
    blocks = pg * PAGE_SIZE // MB_BLOCK
    q = q_ref[0].astype(BF16)
    lane = lax.broadcasted_iota(jnp.int32, (rows, 128), 1).astype(F32)

    def head_match(n_keys):
        r = lax.broadcasted_iota(jnp.int32, (rows, n_keys), 0)
        j = lax.broadcasted_iota(jnp.int32, (rows, n_keys), 1)
        return (j % heads) == (r // t_len), r, j

    @pl.when(p == 0)
    def _():
        cur = jnp.where(lane < n_blocks, gs_ref[0], -jnp.inf)
        sel = jnp.zeros((rows, 128), F32)
        for _ in range(MB_TOPK):
            mx = jnp.max(cur, axis=-1, keepdims=True)
            first = jnp.min(jnp.where(cur == mx, lane, 128.0), axis=-1, keepdims=True)
            hit = lane == first
            sel = jnp.where(hit, 1.0, sel)
            cur = jnp.where(hit, -jnp.inf, cur)
        sel_ref[...] = sel
        m_ref[...] = jnp.full_like(m_ref, NEG_INF)
        l_ref[...] = jnp.zeros_like(l_ref)
        acc_ref[...] = jnp.zeros_like(acc_ref)
        sn = lax.dot_general(q, kn_ref[0].astype(BF16), _NT, preferred_element_type=F32) * scale
        same_head, r, j = head_match(sn.shape[1])
        sn = jnp.where(same_head & ((j // heads) <= (r % t_len)), sn, NEG_INF)
        _online_update(sn, vn_ref[0].astype(BF16), m_ref, l_ref, acc_ref)

    sel = sel_ref[...]
    chosen = []
    for blk in range(blocks):
        block = (p * blocks + blk).astype(F32)
        hit = jnp.sum(jnp.where(lane == block, sel, 0.0), axis=-1, keepdims=True)
        chosen.append(jnp.broadcast_to(hit, (rows, block_keys)))
    s = lax.dot_general(q, _token_head_rows(k_refs), _NT, preferred_element_type=F32) * scale
    same_head, _, _ = head_match(s.shape[1])
    s = jnp.where(same_head & (jnp.concatenate(chosen, axis=-1) > 0.5), s, NEG_INF)
    _online_update(s, _token_head_rows(v_refs), m_ref, l_ref, acc_ref)

    @pl.when(p == pl.num_programs(1) - 1)
    def _():
        o_ref[0] = acc_ref[...] / l_ref[...]


def moba_attn_sample(q, gs, cache_k, cache_v, k_new, v_new, page_table, layer, pg=8):
    db, n_pages = page_table.shape
    rows, dh = q.shape[1:]
    t_len = rows // MB_HEADS
    n_blocks = n_pages * PAGE_SIZE // MB_BLOCK
    kern = functools.partial(_moba_sample_kernel, t_len=t_len, n_blocks=n_blocks, pg=pg)
    pages = _page_specs(layer, pg, MB_HEADS, MB_HEAD_DIM)
    per_b = lambda shape: pl.BlockSpec(shape, lambda b, p, pt: (b, 0, 0))
    grid_spec = pltpu.PrefetchScalarGridSpec(
        num_scalar_prefetch=1,
        grid=(db, n_pages // pg),
        in_specs=[per_b((1, rows, dh)), per_b((1, rows, 128))] + pages + pages + [
            per_b((1, rows, dh)), per_b((1, rows, dh))],
        out_specs=per_b((1, rows, dh)),
        scratch_shapes=[pltpu.VMEM((rows, 128), F32), pltpu.VMEM((rows, 1), F32), pltpu.VMEM((rows, 1), F32),
                        pltpu.VMEM((rows, dh), F32)],
    )
    return pl.pallas_call(
        kern,
        grid_spec=grid_spec,
        out_shape=jax.ShapeDtypeStruct((db, rows, dh), F32),
        compiler_params=_cparams(("parallel", "arbitrary")),
        name="moba_attn_sample",
    )(page_table, q, gs, *([cache_k] * pg), *([cache_v] * pg), k_new, v_new)


def _extract_topk(cur_ref, out_ref, k):
    groups = cur_ref.shape[0]

    def body(j, carry):
        cur = [cur_ref[g] for g in range(groups)]
        mx = [jnp.max(cur[g], axis=0, keepdims=True) for g in range(groups)]
        for g in range(groups):
            out_ref[g, pl.ds(j, 1), :] = mx[g]
            cur_ref[g] = jnp.where(cur[g] == mx[g], -jnp.inf, cur[g])
        return carry

    lax.fori_loop(0, k, body, 0)


_CAND_ROWS = PEER_TOPK + 7 * 8 + 8


def _peer_route_kernel(q_ref, sk_ref, s1m_ref, s2m_ref, eb_ref, a0_ref, thr_ref,
                       cur_ref, top_ref, cand_ref, best_ref):
    half = PEER_KEY_DIM // 2
    k = PEER_TOPK
    for h in range(PEER_HEADS):
        for c, raw_ref in enumerate((s1m_ref, s2m_ref)):
            qc = q_ref[:, (h * 2 + c) * half:(h * 2 + c + 1) * half]
            s = lax.dot_general(sk_ref[h, c], qc, _NT, preferred_element_type=F32,
                                precision=lax.Precision.HIGHEST)
            raw_ref[h] = s
            cur_ref[h * 2 + c] = s
    _extract_topk(cur_ref, top_ref, k)
    for h in range(PEER_HEADS):
        ta = top_ref[2 * h]
        tb = top_ref[2 * h + 1]
        cand_ref[h, 0:k, :] = ta[0:1, :] + tb
        for p in range(1, 8):
            cand_ref[h, k + (p - 1) * 8:k + p * 8, :] = ta[p:p + 1, :] + tb[0:8, :]
        cand_ref[h, k + 56:k + 64, :] = ta[8:16, :] + tb[0:1, :]
    _extract_topk(cand_ref, best_ref, k)
    for h in range(PEER_HEADS):
        ta = top_ref[2 * h]
        tb = top_ref[2 * h + 1]
        best = best_ref[h]
        z = jnp.sum(jnp.exp(best - best[0:1, :]), axis=0, keepdims=True)
        s1 = s1m_ref[h]
        s2 = s2m_ref[h]
        s1m_ref[h] = jnp.where(s1 >= ta[k - 1:k, :], s1, NEG_INF)
        s2m_ref[h] = jnp.where(s2 >= tb[k - 1:k, :], s2, NEG_INF)
        eb_ref[h] = jnp.exp(s2 - tb[0:1, :]) / z
        a0_ref[h:h + 1, :] = ta[0:1, :]
        thr_ref[h:h + 1, :] = best[k - 1:k, :]


def peer_route(q, sub_keys, tm=256):
    m = q.shape[0]
    tm = min(tm, m)
    nk = PEER_N_KEYS
    big = pl.BlockSpec((PEER_HEADS, nk, tm), lambda i: (0, 0, i))
    small = pl.BlockSpec((PEER_HEADS, tm), lambda i: (0, i))
    big_shape = jax.ShapeDtypeStruct((PEER_HEADS, nk, m), F32)
    small_shape = jax.ShapeDtypeStruct((PEER_HEADS, m), F32)
    return pl.pallas_call(
        _peer_route_kernel,
        grid=(m // tm,),
        in_specs=[pl.BlockSpec((tm, PEER_HEADS * PEER_KEY_DIM), lambda i: (i, 0)),
                  pl.BlockSpec(sub_keys.shape, lambda i: (0, 0, 0, 0))],
        out_specs=[big, big, big, small, small],
        out_shape=[big_shape, big_shape, big_shape, small_shape, small_shape],
        scratch_shapes=[pltpu.VMEM((2 * PEER_HEADS, nk, tm), F32),
                        pltpu.VMEM((2 * PEER_HEADS, PEER_TOPK, tm), F32),
                        pltpu.VMEM((PEER_HEADS, _CAND_ROWS, tm), F32),
                        pltpu.VMEM((PEER_HEADS, PEER_TOPK, tm), F32)],
        compiler_params=_cparams(("parallel",)),
        name="peer_route",
    )(q, sub_keys)


def _peer_expert_kernel(x_ref, u_ref, v_ref, s1m_ref, s2m_ref, eb_ref, a0_ref, thr_ref, o_ref, a_ref, *, te):
    e = pl.program_id(1)
    nk = PEER_N_KEYS

    @pl.when(e == 0)
    def _():
        o_ref[...] = jnp.zeros_like(o_ref)

    g = lax.dot_general(u_ref[...], x_ref[...], _NT, preferred_element_type=F32)
    for ii in range(te // nk):
        i = e * (te // nk) + ii
        w = jnp.zeros((nk, g.shape[1]), F32)
        for h in range(PEER_HEADS):
            s1 = s1m_ref[h, pl.ds(i, 1), :]
            ea = jnp.exp(s1 - a0_ref[h:h + 1, :])
            hit = (s1 + s2m_ref[h]) >= thr_ref[h:h + 1, :]
            w = w + jnp.where(hit, ea * eb_ref[h], 0.0)
        gi = g[ii * nk:(ii + 1) * nk, :]
        act = 0.5 * gi * (1.0 + lax.erf(gi * (2.0 ** -0.5)))
        a_ref[ii * nk:(ii + 1) * nk, :] = (w * act).astype(BF16)
    o_ref[...] += lax.dot_general(a_ref[...], v_ref[...], _TN, preferred_element_type=F32)


def peer_experts(x, u, v, layer, s1m, s2m, eb, a0, thr, tm=512, te=1024):
    m, d = x.shape
    tm = min(tm, m)
    n_e = u.shape[1]
    kern = functools.partial(_peer_expert_kernel, te=te)
    once = pl.Buffered(1)
    big = pl.BlockSpec((PEER_HEADS, PEER_N_KEYS, tm), lambda i, e: (0, 0, i), pipeline_mode=once)
    small = pl.BlockSpec((PEER_HEADS, tm), lambda i, e: (0, i), pipeline_mode=once)
    return pl.pallas_call(
        kern,
        grid=(m // tm, n_e // te),
        in_specs=[pl.BlockSpec((tm, d), lambda i, e: (i, 0), pipeline_mode=once),
                  pl.BlockSpec((None, te, d), lambda i, e: (layer, e, 0)),
                  pl.BlockSpec((None, te, d), lambda i, e: (layer, e, 0)),
                  big, big, big, small, small],
        out_specs=pl.BlockSpec((tm, d), lambda i, e: (i, 0), pipeline_mode=once),
        out_shape=jax.ShapeDtypeStruct((m, d), F32),
        scratch_shapes=[pltpu.VMEM((te, tm), BF16)],
        compiler_params=_cparams(("parallel", "arbitrary")),
        name="peer_experts",
    )(x, u, v, s1m, s2m, eb, a0, thr)


def _token_stage(x, x_bf, o_a, o_b, o_c, wts):
    merged = gated_merge(x_bf, o_a, o_b, o_c, wts['w_gate'], wts['b_gate'], wts['w_br_a'], wts['w_br_b'],
                         wts['w_br_c'], wts['layer'])
    h, h_bf = matmul_residual_ln(merged, wts['w_o'], wts['layer'], x, wts['ln1_g'], wts['ln1_b'])
    q = matmul(h_bf, wts['peer_wq'], wts['layer'])
    s1m, s2m, eb, a0, thr = peer_route(q, wts['peer_subkeys'])
    f = peer_experts(h_bf, wts['peer_u'], wts['peer_v'], wts['layer'], s1m, s2m, eb, a0, thr)
    return add_ln(h, f, wts['ln2_g'], wts['ln2_b'])


def kernel(x_prompt, x_sample, cache_diff_k, cache_diff_v, cache_moba_k, cache_moba_v, state_conv, page_table,
           w_in, da_lambda, da_subln_g, conv_w, w_gate, b_gate, w_br_a, w_br_b, w_br_c, w_o, ln1_g, ln1_b,
           peer_wq, peer_subkeys, peer_u, peer_v, ln2_g, ln2_b):
    batch, seq, d = x_prompt.shape
    db, t_len, _ = x_sample.shape
    n_p = batch * seq
    n_s = db * t_len
    pad_s = 128
    ck_a, cv_a = split_component_view(cache_diff_k), split_component_view(cache_diff_v)
    ck_b, cv_b = cache_moba_k, cache_moba_v

    xp = x_prompt.reshape(n_p, d)
    xs = jnp.pad(x_sample.reshape(n_s, d), ((0, pad_s - n_s), (0, 0)))
    xp_bf = xp.astype(BF16)
    xs_bf = xs.astype(BF16)
    w_o_bf, peer_u_bf, peer_v_bf = w_o.astype(BF16), peer_u.astype(BF16), peer_v.astype(BF16)

    st_p = [[] for _ in range(5)]
    st_s = [[] for _ in range(5)]
    for l in range(DEPTH):
        lam_init = 0.8 - 0.6 * math.exp(-0.3 * l)
        wts = {
            'layer': l,
            'w_gate': w_gate, 'b_gate': b_gate.reshape(DEPTH, 1, -1),
            'w_br_a': w_br_a, 'w_br_b': w_br_b, 'w_br_c': w_br_c,
            'w_o': w_o_bf, 'ln1_g': ln1_g[l].reshape(1, -1), 'ln1_b': ln1_b[l].reshape(1, -1),
            'peer_wq': peer_wq, 'peer_subkeys': peer_subkeys[l],
            'peer_u': peer_u_bf, 'peer_v': peer_v_bf,
            'ln2_g': ln2_g[l].reshape(1, -1), 'ln2_b': ln2_b[l].reshape(1, -1),
        }
        lam_p = da_lambda[l]
        sub_g = da_subln_g[l].reshape(1, -1)

        u_p = matmul(xp_bf, w_in, l)
        o_a = diff_attn_prompt(u_p, lam_p, sub_g, batch, seq, lam_init)
        o_b = moba_prompt(u_p, batch, seq)
        o_c, tail = conv_prompt(u_p, conv_w[l], batch, seq)
        xp, xp_bf = _token_stage(xp, xp_bf, o_a, o_b, o_c, wts)
        u3 = u_p.reshape(batch, seq, IN_WIDTH)
        st_p[0].append(u3[:, :, OFF_KA:OFF_KA + DA_K].reshape(batch, seq, DA_KV_HEADS, 2 * DA_HEAD_DIM))
        st_p[1].append(u3[:, :, OFF_VA:OFF_VA + DA_V].reshape(batch, seq, DA_KV_HEADS, 2 * DA_HEAD_DIM))
        st_p[2].append(u3[:, :, OFF_KB:OFF_KB + MB_W].reshape(batch, seq, MB_HEADS, MB_HEAD_DIM))
        st_p[3].append(u3[:, :, OFF_VB:OFF_VB + MB_W].reshape(batch, seq, MB_HEADS, MB_HEAD_DIM))
        st_p[4].append(tail[:, 8 - (CONV_W - 1):, :])

        u_s = matmul(xs_bf, w_in, l)[:n_s].reshape(db, t_len, IN_WIDTH)
        seg = lambda off, width: u_s[:, :, off:off + width]
        q_a = seg(OFF_QA, DA_Q).reshape(db, t_len, DA_KV_HEADS, 2, 2, DA_HEAD_DIM)
        q_a = q_a.transpose(0, 2, 4, 3, 1, 5).reshape(db, 4 * DA_KV_HEADS * t_len, DA_HEAD_DIM)
        k_a, v_a = seg(OFF_KA, DA_K), seg(OFF_VA, DA_V)
        new_rows = lambda a: split_component_view(a.reshape(db, t_len, DA_KV_HEADS, 2 * DA_HEAD_DIM)).reshape(
            db, t_len * 2 * DA_KV_HEADS, DA_HEAD_DIM)
        oa_s = diff_attn_sample(q_a, ck_a, cv_a, new_rows(k_a), new_rows(v_a), page_table, lam_p, sub_g,
                                l, lam_init, t_len)
        oa_s = oa_s.reshape(db, DA_KV_HEADS, 2, t_len, 2 * DA_HEAD_DIM).transpose(0, 3, 1, 2, 4)
        oa_s = oa_s.reshape(n_s, DA_Q)

        q_b = seg(OFF_QB, MB_W).reshape(db, t_len, MB_HEADS, MB_HEAD_DIM)
        k_b, v_b = seg(OFF_KB, MB_W), seg(OFF_VB, MB_W)
        head_major = lambda a: a.transpose(0, 2, 1, 3).reshape(db, MB_HEADS * t_len, a.shape[-1])
        token_head = lambda a: a.reshape(db, t_len * MB_HEADS, MB_HEAD_DIM)
        gs = moba_gate_sample(q_b, ck_b, page_table, l)
        ob_s = moba_attn_sample(head_major(q_b), head_major(gs), ck_b, cv_b, token_head(k_b), token_head(v_b),
                                page_table, l)
        ob_s = ob_s.reshape(db, MB_HEADS, t_len, MB_HEAD_DIM).transpose(0, 2, 1, 3).reshape(n_s, MB_W)

        tmaj = lambda off: seg(off, CONV_DIM).transpose(1, 0, 2)
        oc_s, nst = conv_sample(tmaj(OFF_XC), tmaj(OFF_GB), tmaj(OFF_GC), state_conv[l].transpose(1, 0, 2),
                                conv_w[l])
        oc_s = oc_s.transpose(1, 0, 2).reshape(n_s, CONV_DIM)

        pad_tok = lambda a: jnp.pad(a, ((0, pad_s - n_s), (0, 0))).astype(BF16)
        xs, xs_bf = _token_stage(xs, xs_bf, pad_tok(oa_s), pad_tok(ob_s), pad_tok(oc_s), wts)
        st_s[0].append(k_a.reshape(db, t_len, DA_KV_HEADS, 2 * DA_HEAD_DIM))
        st_s[1].append(v_a.reshape(db, t_len, DA_KV_HEADS, 2 * DA_HEAD_DIM))
        st_s[2].append(k_b.reshape(db, t_len, MB_HEADS, MB_HEAD_DIM))
        st_s[3].append(v_b.reshape(db, t_len, MB_HEADS, MB_HEAD_DIM))
        st_s[4].append(nst.transpose(1, 0, 2))

    y_p = xp.reshape(batch, seq, d)
    y_s = xs[:n_s].reshape(db, t_len, d)
    stack = lambda parts: jnp.stack(parts, axis=0)
    return (y_p, y_s, stack(st_p[0]), stack(st_p[1]), stack(st_p[2]), stack(st_p[3]), stack(st_p[4]),
            stack(st_s[0]), stack(st_s[1]), stack(st_s[2]), stack(st_s[3]), stack(st_s[4]))
```

```python
import functools
import math

import jax
import jax.numpy as jnp
from jax import lax
from jax.experimental import pallas as pl
from jax.experimental.pallas import tpu as pltpu

F32 = jnp.float32
BF16 = jnp.bfloat16

D_MODEL = 4096
DEPTH = 2
PAGE_SIZE = 128
DA_HEADS = 8
DA_KV_HEADS = 4
DA_HEAD_DIM = 128
DA_Q = DA_HEADS * 2 * DA_HEAD_DIM
DA_K = DA_KV_HEADS * 2 * DA_HEAD_DIM
DA_V = DA_K
MB_HEADS = 8
MB_HEAD_DIM = 128
MB_W = MB_HEADS * MB_HEAD_DIM
MB_BLOCK = 256
MB_TOPK = 3
CONV_DIM = 1024
CONV_W = 3
IN_WIDTH = DA_Q + DA_K + DA_V + 3 * MB_W + 3 * CONV_DIM
PEER_HEADS = 8
PEER_N_KEYS = 128
PEER_N_EXPERTS = PEER_N_KEYS * PEER_N_KEYS
PEER_KEY_DIM = 256
PEER_TOPK = 16
ALPHA = (2.0 * DEPTH) ** 0.25
LN_EPS = 1e-5
NEG_INF = -1e30

OFF_QA = 0
OFF_KA = DA_Q
OFF_VA = OFF_KA + DA_K
OFF_QB = OFF_VA + DA_V
OFF_KB = OFF_QB + MB_W
OFF_VB = OFF_KB + MB_W
OFF_XC = OFF_VB + MB_W
OFF_GB = OFF_XC + CONV_DIM
OFF_GC = OFF_GB + CONV_DIM

VMEM_LIMIT = 56 * 1024 * 1024
LANES = 128
SUBLANES = 8

_NT = (((1,), (1,)), ((), ()))
_TN = (((0,), (0,)), ((), ()))


def _cparams(sem):
    return pltpu.CompilerParams(dimension_semantics=sem, vmem_limit_bytes=VMEM_LIMIT)


_ONCE = pl.Buffered(1)


def _mm_kernel(x_ref, w_ref, o_ref, wbf_ref):
    @pl.when(pl.program_id(1) == 0)
    def _():
        wbf_ref[...] = w_ref[...].astype(BF16)

    o_ref[...] = jnp.dot(x_ref[...], wbf_ref[...], preferred_element_type=F32).astype(o_ref.dtype)


def matmul(x, w, layer, out_dtype=F32, tm=1024, tn=512):
    m, k = x.shape
    n = w.shape[2]
    tm = min(tm, m)
    tn = min(tn, n)
    return pl.pallas_call(
        _mm_kernel,
        grid=(n // tn, m // tm),
        in_specs=[pl.BlockSpec((tm, k), lambda j, i: (i, 0)),
                  pl.BlockSpec((None, k, tn), lambda j, i: (layer, 0, j))],
        out_specs=pl.BlockSpec((tm, tn), lambda j, i: (i, j)),
        out_shape=jax.ShapeDtypeStruct((m, n), out_dtype),
        scratch_shapes=[pltpu.VMEM((k, tn), BF16)],
        compiler_params=_cparams(("parallel", "arbitrary")),
        name="matmul",
    )(x, w)


def _merge_kernel(x_ref, oa_ref, ob_ref, oc_ref, wga_ref, wgb_ref, wgc_ref,
                  bga_ref, bgb_ref, bgc_ref, wa_ref, wb_ref, wc_ref, o_ref,
                  wg_bf, wa_bf, wb_bf, wc_bf):
    @pl.when(pl.program_id(1) == 0)
    def _():
        for b, w_r in enumerate((wga_ref, wgb_ref, wgc_ref)):
            wg_bf[b] = w_r[...].astype(BF16)
        wa_bf[...] = wa_ref[...].astype(BF16)
        wb_bf[...] = wb_ref[...].astype(BF16)
        wc_bf[...] = wc_ref[...].astype(BF16)

    x = x_ref[...]
    pre = [jnp.dot(x, wg_bf[b], preferred_element_type=F32) + bg_r[...]
           for b, bg_r in enumerate((bga_ref, bgb_ref, bgc_ref))]
    br = [jnp.dot(o_r[...], w_r[...], preferred_element_type=F32)
          for o_r, w_r in ((oa_ref, wa_bf), (ob_ref, wb_bf), (oc_ref, wc_bf))]
    gates = [jax.nn.sigmoid(z) for z in pre]
    merged = gates[0] * br[0] + gates[1] * br[1] + gates[2] * br[2]
    o_ref[...] = merged.astype(o_ref.dtype)


def gated_merge(x, o_a, o_b, o_c, w_gate, b_gate, w_a, w_b, w_c, layer, tm=512, tn=256):
    m, d = x.shape
    tm = min(tm, m)
    nj = d // tn
    row = lambda j, i: (i, 0)
    panel = lambda rows, off, mode=_ONCE: pl.BlockSpec((None, rows, tn), lambda j, i: (layer, 0, j + off),
                                                       pipeline_mode=mode)
    bias = lambda off: pl.BlockSpec((None, 1, tn), lambda j, i: (layer, 0, j + off))
    return pl.pallas_call(
        _merge_kernel,
        grid=(nj, m // tm),
        in_specs=[pl.BlockSpec((tm, d), row),
                  pl.BlockSpec((tm, o_a.shape[1]), row),
                  pl.BlockSpec((tm, o_b.shape[1]), row),
                  pl.BlockSpec((tm, o_c.shape[1]), row),
                  panel(d, 0), panel(d, nj), panel(d, 2 * nj),
                  bias(0), bias(nj), bias(2 * nj),
                  panel(w_a.shape[1], 0, None), panel(w_b.shape[1], 0, None), panel(w_c.shape[1], 0, None)],
        out_specs=pl.BlockSpec((tm, tn), lambda j, i: (i, j)),
        out_shape=jax.ShapeDtypeStruct((m, d), BF16),
        scratch_shapes=[pltpu.VMEM((3, d, tn), BF16), pltpu.VMEM((w_a.shape[1], tn), BF16),
                        pltpu.VMEM((w_b.shape[1], tn), BF16), pltpu.VMEM((w_c.shape[1], tn), BF16)],
        compiler_params=_cparams(("parallel", "arbitrary")),
        name="gated_merge",
    )(x, o_a, o_b, o_c, w_gate, w_gate, w_gate, b_gate, b_gate, b_gate, w_a, w_b, w_c)


def _layer_norm(z, g, b):
    mu = jnp.mean(z, axis=-1, keepdims=True)
    zc = z - mu
    var = jnp.mean(zc * zc, axis=-1, keepdims=True)
    return zc * lax.rsqrt(var + LN_EPS) * g + b


def _mm_ln_kernel(a_ref, w_ref, res_ref, g_ref, b_ref, o_ref, obf_ref):
    k = pl.program_id(1)

    @pl.when(k == 0)
    def _():
        o_ref[...] = jnp.zeros_like(o_ref)

    o_ref[...] += jnp.dot(a_ref[...], w_ref[...], preferred_element_type=F32)

    @pl.when(k == pl.num_programs(1) - 1)
    def _():
        y = _layer_norm(ALPHA * res_ref[...] + o_ref[...], g_ref[...], b_ref[...])
        o_ref[...] = y
        obf_ref[...] = y.astype(BF16)


def matmul_residual_ln(a, w, layer, res, g, b, tm=512, tk=512):
    m, kdim = a.shape
    d = w.shape[2]
    tm = min(tm, m)
    return pl.pallas_call(
        _mm_ln_kernel,
        grid=(m // tm, kdim // tk),
        in_specs=[pl.BlockSpec((tm, tk), lambda i, k: (i, k)),
                  pl.BlockSpec((None, tk, d), lambda i, k: (layer, k, 0)),
                  pl.BlockSpec((tm, d), lambda i, k: (i, 0), pipeline_mode=pl.Buffered(1)),
                  pl.BlockSpec((1, d), lambda i, k: (0, 0)),
                  pl.BlockSpec((1, d), lambda i, k: (0, 0))],
        out_specs=[pl.BlockSpec((tm, d), lambda i, k: (i, 0)),
                   pl.BlockSpec((tm, d), lambda i, k: (i, 0))],
        out_shape=[jax.ShapeDtypeStruct((m, d), F32), jax.ShapeDtypeStruct((m, d), BF16)],
        compiler_params=_cparams(("parallel", "arbitrary")),
        name="matmul_residual_ln",
    )(a, w, res, g, b)


def _add_ln_kernel(res_ref, f_ref, g_ref, b_ref, o_ref, obf_ref):
    y = _layer_norm(ALPHA * res_ref[...] + f_ref[...], g_ref[...], b_ref[...])
    o_ref[...] = y
    obf_ref[...] = y.astype(BF16)


def add_ln(res, f, g, b, tm=256):
    m, d = res.shape
    tm = min(tm, m)
    spec = pl.BlockSpec((tm, d), lambda i: (i, 0))
    vec = pl.BlockSpec((1, d), lambda i: (0, 0))
    return pl.pallas_call(
        _add_ln_kernel,
        grid=(m // tm,),
        in_specs=[spec, spec, vec, vec],
        out_specs=[spec, spec],
        out_shape=[jax.ShapeDtypeStruct((m, d), F32), jax.ShapeDtypeStruct((m, d), BF16)],
        compiler_params=_cparams(("parallel",)),
        name="add_ln",
    )(res, f, g, b)


def _lambda_value(lp, lam_init):
    a = jnp.sum(lp[0:1, :] * lp[1:2, :], axis=-1, keepdims=True)
    b = jnp.sum(lp[2:3, :] * lp[3:4, :], axis=-1, keepdims=True)
    return jnp.exp(a) - jnp.exp(b) + lam_init


def _online_update(s, v_bf, m_ref, l_ref, acc_ref):
    m_prev = m_ref[...]
    m_new = jnp.maximum(m_prev, jnp.max(s, axis=-1, keepdims=True))
    alpha = jnp.exp(m_prev - m_new)
    p = jnp.exp(s - m_new)
    l_ref[...] = alpha * l_ref[...] + jnp.sum(p, axis=-1, keepdims=True)
    acc_ref[...] = alpha * acc_ref[...] + jnp.dot(p.astype(BF16), v_bf, preferred_element_type=F32)
    m_ref[...] = m_new


def _subln(o, g, lam_init):
    return o * lax.rsqrt(jnp.mean(o * o, axis=-1, keepdims=True) + LN_EPS) * g * (1.0 - lam_init)


def _online_update_keymajor(scores, vts, m_ref, l_ref, acc_ref):
    chains = range(len(scores))
    m_prev = [m_ref[c] for c in chains]
    m_new = [jnp.maximum(m_prev[c], jnp.max(scores[c], axis=0, keepdims=True)) for c in chains]
    alpha = [jnp.exp(m_prev[c] - m_new[c]) for c in chains]
    p = [jnp.exp(scores[c] - m_new[c]) for c in chains]
    for c in chains:
        l_ref[c] = alpha[c] * l_ref[c] + jnp.sum(p[c], axis=0, keepdims=True)
        m_ref[c] = m_new[c]
    pv = [jnp.dot(vts[c], p[c].astype(BF16), preferred_element_type=F32) for c in chains]
    for c in chains:
        acc_ref[c] = alpha[c] * acc_ref[c] + pv[c]


def _diff_prompt_kernel(q_ref, k_ref, v_ref, lam_ref, g_ref, o_ref, vt_ref, m_ref, l_ref, acc_ref, *,
                        lam_init, tq):
    i = pl.program_id(2)
    dh = DA_HEAD_DIM
    scale = dh ** -0.5
    n_kv = v_ref.shape[0] // tq

    @pl.when(i == 0)
    def _():
        for j in range(n_kv):
            vt_ref[j] = v_ref[j * tq:(j + 1) * tq, :].T.astype(BF16)

    lam = _lambda_value(lam_ref[...], lam_init)
    key = lax.broadcasted_iota(jnp.int32, (tq, 2 * tq), 0)
    qry = lax.broadcasted_iota(jnp.int32, (tq, 2 * tq), 1)
    causal = key <= jnp.where(qry >= tq, qry - tq, qry)
    qs = [(jnp.concatenate([q_ref[:, c * dh:(c + 1) * dh], q_ref[:, (2 + c) * dh:(3 + c) * dh]], axis=0)
           * scale).astype(BF16) for c in range(2)]
    m_ref[...] = jnp.full_like(m_ref, NEG_INF)
    l_ref[...] = jnp.zeros_like(l_ref)
    acc_ref[...] = jnp.zeros_like(acc_ref)

    def tile(j, masked):
        start = pl.multiple_of(j * tq, tq)
        vt = vt_ref[j]
        scores = []
        for c in range(2):
            kt = k_ref[pl.ds(start, tq), c * dh:(c + 1) * dh].astype(BF16)
            s = lax.dot_general(kt, qs[c], _NT, preferred_element_type=F32)
            scores.append(jnp.where(causal, s, NEG_INF) if masked else s)
        _online_update_keymajor(scores, [vt, vt], m_ref, l_ref, acc_ref)

    def body(j, carry):
        tile(j, False)
        return carry

    lax.fori_loop(0, i, body, 0)
    tile(i, True)
    ot = acc_ref[0] / l_ref[0] - lam * (acc_ref[1] / l_ref[1])
    ot = ot * lax.rsqrt(jnp.mean(ot * ot, axis=0, keepdims=True) + LN_EPS) * g_ref[...] * (1.0 - lam_init)
    o = ot.T.astype(o_ref.dtype)
    o_ref[:, 0:2 * dh] = o[:tq]
    o_ref[:, 2 * dh:4 * dh] = o[tq:]


def diff_attn_prompt(u, lam_p, subln_g, batch, seq, lam_init, tq=512):
    nq = seq // tq
    qw = 4 * DA_HEAD_DIM
    kw = 2 * DA_HEAD_DIM
    kern = functools.partial(_diff_prompt_kernel, lam_init=lam_init, tq=tq)
    return pl.pallas_call(
        kern,
        grid=(batch, DA_KV_HEADS, nq),
        in_specs=[pl.BlockSpec((tq, qw), lambda b, k, i: (b * nq + i, k)),
                  pl.BlockSpec((seq, kw), lambda b, k, i: (b, OFF_KA // kw + k)),
                  pl.BlockSpec((seq, kw), lambda b, k, i: (b, OFF_VA // kw + k)),
                  pl.BlockSpec((4, DA_HEAD_DIM), lambda b, k, i: (0, 0)),
                  pl.BlockSpec((kw, 1), lambda b, k, i: (0, 0))],
        out_specs=pl.BlockSpec((tq, qw), lambda b, k, i: (b * nq + i, k)),
        out_shape=jax.ShapeDtypeStruct((batch * seq, DA_Q), BF16),
        scratch_shapes=[pltpu.VMEM((nq, kw, tq), BF16),
                        pltpu.VMEM((2, 1, 2 * tq), F32), pltpu.VMEM((2, 1, 2 * tq), F32),
                        pltpu.VMEM((2, kw, 2 * tq), F32)],
        compiler_params=_cparams(("parallel", "parallel", "arbitrary")),
        name="diff_attn_prompt",
    )(u, u, u, lam_p, subln_g.reshape(kw, 1))


def _moba_prompt_kernel(q_ref, k_ref, v_ref, o_ref, km_ref, vt_ref, sel_ref, m_ref, l_ref, acc_ref, *, nb, hg):
    i = pl.program_id(2)
    blk = MB_BLOCK
    dh = MB_HEAD_DIM
    scale = dh ** -0.5

    @pl.when(i == 0)
    def _():
        km_ref[...] = jnp.zeros_like(km_ref)
        for n in range(nb):
            km_ref[n:n + 1, :] = jnp.mean(k_ref[n * blk:(n + 1) * blk, :], axis=0, keepdims=True)
            for h in range(hg):
                vt_ref[h, n] = v_ref[n * blk:(n + 1) * blk, h * dh:(h + 1) * dh].T.astype(BF16)

    heads = range(hg)
    qf = [q_ref[:, h * dh:(h + 1) * dh] for h in heads]
    qs = [(qf[h] * scale).astype(BF16) for h in heads]
    gs = [lax.dot_general(km_ref[:, h * dh:(h + 1) * dh], qf[h], _NT, preferred_element_type=F32,
                          precision=lax.Precision.HIGHEST) for h in heads]
    block = lax.broadcasted_iota(jnp.int32, gs[0].shape, 0)
    past = block < i
    for n in range(nb):
        ahead = [(gs[h] > gs[h][n:n + 1, :]) | ((gs[h] == gs[h][n:n + 1, :]) & (block < n)) for h in heads]
        rank = [jnp.sum(jnp.where(past & ahead[h], 1.0, 0.0), axis=0, keepdims=True) for h in heads]
        for h in heads:
            sel_ref[h, n] = jnp.where(rank[h] < MB_TOPK, 1.0, 0.0)

    key = lax.broadcasted_iota(jnp.int32, (blk, blk), 0)
    qry = lax.broadcasted_iota(jnp.int32, (blk, blk), 1)
    m_ref[...] = jnp.full_like(m_ref, NEG_INF)
    l_ref[...] = jnp.zeros_like(l_ref)
    acc_ref[...] = jnp.zeros_like(acc_ref)

    def tile(n, mask_of):
        start = pl.multiple_of(n * blk, blk)
        scores = []
        for h in range(hg):
            kt = k_ref[pl.ds(start, blk), h * dh:(h + 1) * dh].astype(BF16)
            s = lax.dot_general(kt, qs[h], _NT, preferred_element_type=F32)
            scores.append(jnp.where(mask_of(h), s, NEG_INF))
        _online_update_keymajor(scores, [vt_ref[h, n] for h in range(hg)], m_ref, l_ref, acc_ref)

    tile(i, lambda h: key <= qry)

    def body(n, carry):
        tile(n, lambda h: sel_ref[h, n] > 0.5)
        return carry

    lax.fori_loop(0, i, body, 0)
    for h in range(hg):
        o_ref[:, h * dh:(h + 1) * dh] = (acc_ref[h] / l_ref[h]).T.astype(o_ref.dtype)


def moba_prompt(u, batch, seq, hg=MB_HEADS):
    nb = seq // MB_BLOCK
    dh = MB_HEAD_DIM
    w = hg * dh
    kern = functools.partial(_moba_prompt_kernel, nb=nb, hg=hg)
    once = pl.Buffered(1)
    return pl.pallas_call(
        kern,
        grid=(batch, MB_HEADS // hg, nb),
        in_specs=[pl.BlockSpec((MB_BLOCK, w), lambda b, h, i: (b * nb + i, OFF_QB // w + h)),
                  pl.BlockSpec((seq, w), lambda b, h, i: (b, OFF_KB // w + h), pipeline_mode=once),
                  pl.BlockSpec((seq, w), lambda b, h, i: (b, OFF_VB // w + h), pipeline_mode=once)],
        out_specs=pl.BlockSpec((MB_BLOCK, w), lambda b, h, i: (b * nb + i, h)),
        out_shape=jax.ShapeDtypeStruct((batch * seq, MB_W), BF16),
        scratch_shapes=[pltpu.VMEM((-(-nb // SUBLANES) * SUBLANES, w), F32),
                        pltpu.VMEM((hg, nb, dh, MB_BLOCK), BF16),
                        pltpu.VMEM((hg, nb, 1, MB_BLOCK), F32),
                        pltpu.VMEM((hg, 1, MB_BLOCK), F32), pltpu.VMEM((hg, 1, MB_BLOCK), F32),
                        pltpu.VMEM((hg, dh, MB_BLOCK), F32)],
        compiler_params=_cparams(("parallel", "parallel", "arbitrary")),
        name="moba_prompt",
    )(u, u, u)


def _conv_prompt_kernel(xc_ref, gb_ref, gc_ref, w_ref, o_ref, st_ref):
    z = gc_ref[...] * xc_ref[...]
    t = z.shape[0]
    row = lax.broadcasted_iota(jnp.int32, z.shape, 0)
    z1 = jnp.where(row >= 1, pltpu.roll(z, 1, 0), 0.0)
    z2 = jnp.where(row >= 2, pltpu.roll(z, 2, 0), 0.0)
    y = w_ref[0:1, :] * z2 + w_ref[1:2, :] * z1 + w_ref[2:3, :] * z
    o_ref[...] = (gb_ref[...] * y).astype(o_ref.dtype)
    st_ref[0] = z[t - SUBLANES:, :]


def conv_prompt(u, conv_w, batch, seq, tc=256):
    nc = CONV_DIM // tc
    blk = lambda off: pl.BlockSpec((seq, tc), lambda b, c: (b, off // tc + c))
    return pl.pallas_call(
        _conv_prompt_kernel,
        grid=(batch, nc),
        in_specs=[blk(OFF_XC), blk(OFF_GB), blk(OFF_GC), pl.BlockSpec((CONV_W, tc), lambda b, c: (0, c))],
        out_specs=[pl.BlockSpec((seq, tc), lambda b, c: (b, c)),
                   pl.BlockSpec((1, SUBLANES, tc), lambda b, c: (b, 0, c))],
        out_shape=[jax.ShapeDtypeStruct((batch * seq, CONV_DIM), BF16),
                   jax.ShapeDtypeStruct((batch, SUBLANES, CONV_DIM), F32)],
        compiler_params=_cparams(("parallel", "parallel")),
        name="conv_prompt",
    )(u, u, u, conv_w)


def _conv_sample_kernel(xc_ref, gb_ref, gc_ref, st_ref, w_ref, o_ref, nst_ref):
    t_len = xc_ref.shape[0]
    zz = [st_ref[0], st_ref[1]] + [gc_ref[t] * xc_ref[t] for t in range(t_len)]
    for t in range(t_len):
        y = w_ref[0:1, :] * zz[t] + w_ref[1:2, :] * zz[t + 1] + w_ref[2:3, :] * zz[t + 2]
        o_ref[t] = gb_ref[t] * y
    nst_ref[0] = zz[t_len]
    nst_ref[1] = zz[t_len + 1]


def conv_sample(xc, gb, gc, state, conv_w):
    return pl.pallas_call(
        _conv_sample_kernel,
        out_shape=[jax.ShapeDtypeStruct(xc.shape, F32), jax.ShapeDtypeStruct(state.shape, F32)],
        name="conv_sample",
    )(xc, gb, gc, state, conv_w)


def _page_specs(layer, pg, heads, dh):
    return [pl.BlockSpec((None, None, PAGE_SIZE, heads, dh),
                         lambda b, p, pt, j=j: (layer, pt[b, p * pg + j], 0, 0, 0)) for j in range(pg)]


def _token_head_rows(page_refs, dtype=BF16):
    ps, heads, dh = page_refs[0].shape
    return jnp.concatenate([r[...].reshape(ps * heads, dh).astype(dtype) for r in page_refs], axis=0)


def split_component_view(x):
    lead = x.shape[:-2]
    kv, w = x.shape[-2:]
    n = len(lead)
    x = x.reshape(*lead, kv, 2, w // 2)
    return x.transpose(*range(n), n + 1, n, n + 2).reshape(*lead, 2 * kv, w // 2)


def _swap_row_halves(x, group):
    n, d = x.shape
    return pltpu.roll(x.reshape(n // group, group, d), group // 2, 1).reshape(n, d)


def _diff_sample_kernel(pt_ref, q_ref, *refs, lam_init, t_len, pg):
    k_refs, v_refs = refs[:pg], refs[pg:2 * pg]
    kn_ref, vn_ref, lam_ref, g_ref, o_ref, m_ref, l_ref, acc_ref = refs[2 * pg:]
    p = pl.program_id(1)
    dh = DA_HEAD_DIM
    kv = DA_KV_HEADS
    scale = dh ** -0.5
    per_comp = 2 * t_len
    rows = kv * 2 * per_comp
    q = q_ref[0].astype(BF16)

    def own_rows(n_keys):
        r = lax.broadcasted_iota(jnp.int32, (rows, n_keys), 0)
        j = lax.broadcasted_iota(jnp.int32, (rows, n_keys), 1)
        target = ((r // per_comp) % 2) * kv + r // (2 * per_comp)
        return (j % (2 * kv)) == target, r, j

    def values(v_f32):
        return jnp.concatenate([v_f32, _swap_row_halves(v_f32, 2 * kv)], axis=-1).astype(BF16)

    @pl.when(p == 0)
    def _():
        m_ref[...] = jnp.full_like(m_ref, NEG_INF)
        l_ref[...] = jnp.zeros_like(l_ref)
        acc_ref[...] = jnp.zeros_like(acc_ref)

    s = lax.dot_general(q, _token_head_rows(k_refs), _NT, preferred_element_type=F32) * scale
    own, _, _ = own_rows(s.shape[1])
    s = jnp.where(own, s, NEG_INF)
    _online_update(s, values(_token_head_rows(v_refs, F32)), m_ref, l_ref, acc_ref)

    @pl.when(p == pl.num_programs(1) - 1)
    def _():
        sn = lax.dot_general(q, kn_ref[0].astype(BF16), _NT, preferred_element_type=F32) * scale
        own_n, r, j = own_rows(sn.shape[1])
        sn = jnp.where(own_n & ((j // (2 * kv)) <= (r % t_len)), sn, NEG_INF)
        _online_update(sn, values(vn_ref[0]), m_ref, l_ref, acc_ref)
        o = acc_ref[...] / l_ref[...]
        lam = _lambda_value(lam_ref[...], lam_init)
        for k in range(kv):
            base = k * 2 * per_comp
            o1 = o[base:base + per_comp, :]
            o2 = o[base + per_comp:base + 2 * per_comp, :]
            o2 = jnp.concatenate([o2[:, dh:], o2[:, :dh]], axis=-1)
            o_ref[0, k] = _subln(o1 - lam * o2, g_ref[...], lam_init)


def diff_attn_sample(q, cache_k, cache_v, k_new, v_new, page_table, lam_p, subln_g, layer, lam_init, t_len,
                     pg=8):
    db, n_pages = page_table.shape
    rows, dh = q.shape[1:]
    per_comp = 2 * t_len
    kern = functools.partial(_diff_sample_kernel, lam_init=lam_init, t_len=t_len, pg=pg)
    pages = _page_specs(layer, pg, 2 * DA_KV_HEADS, dh)
    per_b = lambda shape: pl.BlockSpec(shape, lambda b, p, pt: (b, 0, 0))
    n_new = k_new.shape[1]
    grid_spec = pltpu.PrefetchScalarGridSpec(
        num_scalar_prefetch=1,
        grid=(db, n_pages // pg),
        in_specs=[per_b((1, rows, dh))] + pages + pages + [
            per_b((1, n_new, dh)), per_b((1, n_new, dh)),
            pl.BlockSpec((4, dh), lambda b, p, pt: (0, 0)),
            pl.BlockSpec((1, 2 * dh), lambda b, p, pt: (0, 0))],
        out_specs=pl.BlockSpec((1, DA_KV_HEADS, per_comp, 2 * dh), lambda b, p, pt: (b, 0, 0, 0)),
        scratch_shapes=[pltpu.VMEM((rows, 1), F32), pltpu.VMEM((rows, 1), F32), pltpu.VMEM((rows, 2 * dh), F32)],
    )
    return pl.pallas_call(
        kern,
        grid_spec=grid_spec,
        out_shape=jax.ShapeDtypeStruct((db, DA_KV_HEADS, per_comp, 2 * dh), F32),
        compiler_params=_cparams(("parallel", "arbitrary")),
        name="diff_attn_sample",
    )(page_table, q, *([cache_k] * pg), *([cache_v] * pg), k_new, v_new, lam_p, subln_g)


def _moba_gate_kernel(pt_ref, q_ref, *refs, pg, t_len):
    k_refs, gs_ref = refs[:pg], refs[pg]
    p = pl.program_id(1)
    pages_per_block = MB_BLOCK // PAGE_SIZE
    blocks = pg // pages_per_block

    @pl.when(p == 0)
    def _():
        gs_ref[...] = jnp.zeros_like(gs_ref)

    lane = lax.broadcasted_iota(jnp.int32, gs_ref.shape[2:], 1)
    for blk in range(blocks):
        page_sums = [jnp.sum(k_refs[blk * pages_per_block + j][...], axis=0) for j in range(pages_per_block)]
        km = functools.reduce(lambda a, b: a + b, page_sums) / MB_BLOCK
        n = p * blocks + blk
        for t in range(t_len):
            g = jnp.sum(q_ref[0, t] * km, axis=-1, keepdims=True)
            gs_ref[0, t] += jnp.where(lane == n, g, 0.0)


def moba_gate_sample(q, cache_k, page_table, layer, pg=8):
    db, n_pages = page_table.shape
    t_len = q.shape[1]
    kern = functools.partial(_moba_gate_kernel, pg=pg, t_len=t_len)
    grid_spec = pltpu.PrefetchScalarGridSpec(
        num_scalar_prefetch=1,
        grid=(db, n_pages // pg),
        in_specs=[pl.BlockSpec((1, t_len, MB_HEADS, MB_HEAD_DIM), lambda b, p, pt: (b, 0, 0, 0))]
        + _page_specs(layer, pg, MB_HEADS, MB_HEAD_DIM),
        out_specs=pl.BlockSpec((1, t_len, MB_HEADS, LANES), lambda b, p, pt: (b, 0, 0, 0)),
    )
    return pl.pallas_call(
        kern,
        grid_spec=grid_spec,
        out_shape=jax.ShapeDtypeStruct((db, t_len, MB_HEADS, LANES), F32),
        compiler_params=_cparams(("parallel", "arbitrary")),
        name="moba_gate_sample",
    )(page_table, q, *([cache_k] * pg))


def _moba_sample_kernel(pt_ref, q_ref, gs_ref, *refs, t_len, n_blocks, pg):
    k_refs, v_refs = refs[:pg], refs[pg:2 * pg]
    kn_ref, vn_ref, o_ref, sel_ref, m_ref, l_ref, acc_ref = refs[2 * pg:]
    p = pl.program_id(1)
    scale = MB_HEAD_DIM ** -0.5
    heads = MB_HEADS
    rows = heads * t_len
    block_keys = MB_BLOCK * heads
    blocks = pg * PAGE_SIZE // MB_BLOCK
    q = q_ref[0].astype(BF16)
    lane = lax.broadcasted_iota(jnp.int32, (rows, LANES), 1).astype(F32)

    def head_match(n_keys):
        r = lax.broadcasted_iota(jnp.int32, (rows, n_keys), 0)
        j = lax.broadcasted_iota(jnp.int32, (rows, n_keys), 1)
        return (j % heads) == (r // t_len), r, j

    @pl.when(p == 0)
    def _():
        cur = jnp.where(lane < n_blocks, gs_ref[0], -jnp.inf)
        sel = jnp.zeros((rows, LANES), F32)
        for _ in range(MB_TOPK):
            mx = jnp.max(cur, axis=-1, keepdims=True)
            first = jnp.min(jnp.where(cur == mx, lane, float(LANES)), axis=-1, keepdims=True)
            hit = lane == first
            sel = jnp.where(hit, 1.0, sel)
            cur = jnp.where(hit, -jnp.inf, cur)
        sel_ref[...] = sel
        m_ref[...] = jnp.full_like(m_ref, NEG_INF)
        l_ref[...] = jnp.zeros_like(l_ref)
        acc_ref[...] = jnp.zeros_like(acc_ref)
        sn = lax.dot_general(q, kn_ref[0].astype(BF16), _NT, preferred_element_type=F32) * scale
        same_head, r, j = head_match(sn.shape[1])
        sn = jnp.where(same_head & ((j // heads) <= (r % t_len)), sn, NEG_INF)
        _online_update(sn, vn_ref[0].astype(BF16), m_ref, l_ref, acc_ref)

    sel = sel_ref[...]
    chosen = []
    for blk in range(blocks):
        block = (p * blocks + blk).astype(F32)
        hit = jnp.sum(jnp.where(lane == block, sel, 0.0), axis=-1, keepdims=True)
        chosen.append(jnp.broadcast_to(hit, (rows, block_keys)))
    s = lax.dot_general(q, _token_head_rows(k_refs), _NT, preferred_element_type=F32) * scale
    same_head, _, _ = head_match(s.shape[1])
    s = jnp.where(same_head & (jnp.concatenate(chosen, axis=-1) > 0.5), s, NEG_INF)
    _online_update(s, _token_head_rows(v_refs), m_ref, l_ref, acc_ref)

    @pl.when(p == pl.num_programs(1) - 1)
    def _():
        o_ref[0] = acc_ref[...] / l_ref[...]


def moba_attn_sample(q, gs, cache_k, cache_v, k_new, v_new, page_table, layer, pg=8):
    db, n_pages = page_table.shape
    rows, dh = q.shape[1:]
    t_len = rows // MB_HEADS
    n_blocks = n_pages * PAGE_SIZE // MB_BLOCK
    assert n_blocks <= LANES and n_pages % pg == 0 and (pg * PAGE_SIZE) % MB_BLOCK == 0
    kern = functools.partial(_moba_sample_kernel, t_len=t_len, n_blocks=n_blocks, pg=pg)
    pages = _page_specs(layer, pg, MB_HEADS, MB_HEAD_DIM)
    per_b = lambda shape: pl.BlockSpec(shape, lambda b, p, pt: (b, 0, 0))
    grid_spec = pltpu.PrefetchScalarGridSpec(
        num_scalar_prefetch=1,
        grid=(db, n_pages // pg),
        in_specs=[per_b((1, rows, dh)), per_b((1, rows, LANES))] + pages + pages + [
            per_b((1, rows, dh)), per_b((1, rows, dh))],
        out_specs=per_b((1, rows, dh)),
        scratch_shapes=[pltpu.VMEM((rows, LANES), F32), pltpu.VMEM((rows, 1), F32), pltpu.VMEM((rows, 1), F32),
                        pltpu.VMEM((rows, dh), F32)],
    )
    return pl.pallas_call(
        kern,
        grid_spec=grid_spec,
        out_shape=jax.ShapeDtypeStruct((db, rows, dh), F32),
        compiler_params=_cparams(("parallel", "arbitrary")),
        name="moba_attn_sample",
    )(page_table, q, gs, *([cache_k] * pg), *([cache_v] * pg), k_new, v_new)


def _extract_topk(cur_ref, out_ref, k):
    groups = cur_ref.shape[0]

    def body(j, carry):
        cur = [cur_ref[g] for g in range(groups)]
        mx = [jnp.max(cur[g], axis=0, keepdims=True) for g in range(groups)]
        for g in range(groups):
            out_ref[g, pl.ds(j, 1), :] = mx[g]
            cur_ref[g] = jnp.where(cur[g] == mx[g], -jnp.inf, cur[g])
        return carry

    lax.fori_loop(0, k, body, 0)


_HALF_TOPK = PEER_TOPK // 2
_CAND_ROWS = PEER_TOPK + (_HALF_TOPK - 1) * _HALF_TOPK + _HALF_TOPK


def _peer_route_kernel(q_ref, sk_ref, s1m_ref, s2m_ref, eb_ref, a0_ref, thr_ref,
                       cur_ref, top_ref, cand_ref, best_ref):
    half = PEER_KEY_DIM // 2
    k = PEER_TOPK
    for h in range(PEER_HEADS):
        for c, raw_ref in enumerate((s1m_ref, s2m_ref)):
            qc = q_ref[:, (h * 2 + c) * half:(h * 2 + c + 1) * half]
            s = lax.dot_general(sk_ref[h, c], qc, _NT, preferred_element_type=F32,
                                precision=lax.Precision.HIGHEST)
            raw_ref[h] = s
            cur_ref[h * 2 + c] = s
    _extract_topk(cur_ref, top_ref, k)
    for h in range(PEER_HEADS):
        ta = top_ref[2 * h]
        tb = top_ref[2 * h + 1]
        hk = _HALF_TOPK
        cand_ref[h, 0:k, :] = ta[0:1, :] + tb
        for p in range(1, hk):
            cand_ref[h, k + (p - 1) * hk:k + p * hk, :] = ta[p:p + 1, :] + tb[0:hk, :]
        cand_ref[h, k + (hk - 1) * hk:_CAND_ROWS, :] = ta[hk:k, :] + tb[0:1, :]
    _extract_topk(cand_ref, best_ref, k)
    for h in range(PEER_HEADS):
        ta = top_ref[2 * h]
        tb = top_ref[2 * h + 1]
        best = best_ref[h]
        z = jnp.sum(jnp.exp(best - best[0:1, :]), axis=0, keepdims=True)
        s1 = s1m_ref[h]
        s2 = s2m_ref[h]
        s1m_ref[h] = jnp.where(s1 >= ta[k - 1:k, :], s1, NEG_INF)
        s2m_ref[h] = jnp.where(s2 >= tb[k - 1:k, :], s2, NEG_INF)
        eb_ref[h] = jnp.exp(s2 - tb[0:1, :]) / z
        a0_ref[h:h + 1, :] = ta[0:1, :]
        thr_ref[h:h + 1, :] = best[k - 1:k, :]


def peer_route(q, sub_keys, tm=256):
    m = q.shape[0]
    tm = min(tm, m)
    nk = PEER_N_KEYS
    big = pl.BlockSpec((PEER_HEADS, nk, tm), lambda i: (0, 0, i))
    small = pl.BlockSpec((PEER_HEADS, tm), lambda i: (0, i))
    big_shape = jax.ShapeDtypeStruct((PEER_HEADS, nk, m), F32)
    small_shape = jax.ShapeDtypeStruct((PEER_HEADS, m), F32)
    return pl.pallas_call(
        _peer_route_kernel,
        grid=(m // tm,),
        in_specs=[pl.BlockSpec((tm, PEER_HEADS * PEER_KEY_DIM), lambda i: (i, 0)),
                  pl.BlockSpec(sub_keys.shape, lambda i: (0, 0, 0, 0))],
        out_specs=[big, big, big, small, small],
        out_shape=[big_shape, big_shape, big_shape, small_shape, small_shape],
        scratch_shapes=[pltpu.VMEM((2 * PEER_HEADS, nk, tm), F32),
                        pltpu.VMEM((2 * PEER_HEADS, PEER_TOPK, tm), F32),
                        pltpu.VMEM((PEER_HEADS, _CAND_ROWS, tm), F32),
                        pltpu.VMEM((PEER_HEADS, PEER_TOPK, tm), F32)],
        compiler_params=_cparams(("parallel",)),
        name="peer_route",
    )(q, sub_keys)


def _peer_expert_kernel(x_ref, u_ref, v_ref, s1m_ref, s2m_ref, eb_ref, a0_ref, thr_ref, o_ref, a_ref, *, te):
    e = pl.program_id(1)
    nk = PEER_N_KEYS

    @pl.when(e == 0)
    def _():
        o_ref[...] = jnp.zeros_like(o_ref)

    g = lax.dot_general(u_ref[...], x_ref[...], _NT, preferred_element_type=F32)
    for ii in range(te // nk):
        i = e * (te // nk) + ii
        w = jnp.zeros((nk, g.shape[1]), F32)
        for h in range(PEER_HEADS):
            s1 = s1m_ref[h, pl.ds(i, 1), :]
            ea = jnp.exp(s1 - a0_ref[h:h + 1, :])
            hit = (s1 + s2m_ref[h]) >= thr_ref[h:h + 1, :]
            w = w + jnp.where(hit, ea * eb_ref[h], 0.0)
        gi = g[ii * nk:(ii + 1) * nk, :]
        act = 0.5 * gi * (1.0 + lax.erf(gi * (2.0 ** -0.5)))
        a_ref[ii * nk:(ii + 1) * nk, :] = (w * act).astype(BF16)
    o_ref[...] += lax.dot_general(a_ref[...], v_ref[...], _TN, preferred_element_type=F32)


def peer_experts(x, u, v, layer, s1m, s2m, eb, a0, thr, tm=512, te=1024):
    m, d = x.shape
    tm = min(tm, m)
    n_e = u.shape[1]
    kern = functools.partial(_peer_expert_kernel, te=te)
    once = pl.Buffered(1)
    big = pl.BlockSpec((PEER_HEADS, PEER_N_KEYS, tm), lambda i, e: (0, 0, i), pipeline_mode=once)
    small = pl.BlockSpec((PEER_HEADS, tm), lambda i, e: (0, i), pipeline_mode=once)
    return pl.pallas_call(
        kern,
        grid=(m // tm, n_e // te),
        in_specs=[pl.BlockSpec((tm, d), lambda i, e: (i, 0), pipeline_mode=once),
                  pl.BlockSpec((None, te, d), lambda i, e: (layer, e, 0)),
                  pl.BlockSpec((None, te, d), lambda i, e: (layer, e, 0)),
                  big, big, big, small, small],
        out_specs=pl.BlockSpec((tm, d), lambda i, e: (i, 0), pipeline_mode=once),
        out_shape=jax.ShapeDtypeStruct((m, d), F32),
        scratch_shapes=[pltpu.VMEM((te, tm), BF16)],
        compiler_params=_cparams(("parallel", "arbitrary")),
        name="peer_experts",
    )(x, u, v, s1m, s2m, eb, a0, thr)


def _token_stage(x, x_bf, o_a, o_b, o_c, wts):
    merged = gated_merge(x_bf, o_a, o_b, o_c, wts['w_gate'], wts['b_gate'], wts['w_br_a'], wts['w_br_b'],
                         wts['w_br_c'], wts['layer'])
    h, h_bf = matmul_residual_ln(merged, wts['w_o'], wts['layer'], x, wts['ln1_g'], wts['ln1_b'])
    q = matmul(h_bf, wts['peer_wq'], wts['layer'])
    s1m, s2m, eb, a0, thr = peer_route(q, wts['peer_subkeys'])
    f = peer_experts(h_bf, wts['peer_u'], wts['peer_v'], wts['layer'], s1m, s2m, eb, a0, thr)
    return add_ln(h, f, wts['ln2_g'], wts['ln2_b'])


def kernel(x_prompt, x_sample, cache_diff_k, cache_diff_v, cache_moba_k, cache_moba_v, state_conv, page_table,
           w_in, da_lambda, da_subln_g, conv_w, w_gate, b_gate, w_br_a, w_br_b, w_br_c, w_o, ln1_g, ln1_b,
           peer_wq, peer_subkeys, peer_u, peer_v, ln2_g, ln2_b):
    batch, seq, d = x_prompt.shape
    db, t_len, _ = x_sample.shape
    n_p = batch * seq
    n_s = db * t_len
    pad_s = LANES
    ck_a, cv_a = split_component_view(cache_diff_k), split_component_view(cache_diff_v)
    ck_b, cv_b = cache_moba_k, cache_moba_v

    xp = x_prompt.reshape(n_p, d)
    xs = jnp.pad(x_sample.reshape(n_s, d), ((0, pad_s - n_s), (0, 0)))
    xp_bf = xp.astype(BF16)
    xs_bf = xs.astype(BF16)
    w_o_bf, peer_u_bf, peer_v_bf = w_o.astype(BF16), peer_u.astype(BF16), peer_v.astype(BF16)

    st_p = [[] for _ in range(5)]
    st_s = [[] for _ in range(5)]
    for l in range(DEPTH):
        lam_init = 0.8 - 0.6 * math.exp(-0.3 * l)
        wts = {
            'layer': l,
            'w_gate': w_gate, 'b_gate': b_gate.reshape(DEPTH, 1, -1),
            'w_br_a': w_br_a, 'w_br_b': w_br_b, 'w_br_c': w_br_c,
            'w_o': w_o_bf, 'ln1_g': ln1_g[l].reshape(1, -1), 'ln1_b': ln1_b[l].reshape(1, -1),
            'peer_wq': peer_wq, 'peer_subkeys': peer_subkeys[l],
            'peer_u': peer_u_bf, 'peer_v': peer_v_bf,
            'ln2_g': ln2_g[l].reshape(1, -1), 'ln2_b': ln2_b[l].reshape(1, -1),
        }
        lam_p = da_lambda[l]
        sub_g = da_subln_g[l].reshape(1, -1)

        u_p = matmul(xp_bf, w_in, l)
        o_a = diff_attn_prompt(u_p, lam_p, sub_g, batch, seq, lam_init)
        o_b = moba_prompt(u_p, batch, seq)
        o_c, tail = conv_prompt(u_p, conv_w[l], batch, seq)
        xp, xp_bf = _token_stage(xp, xp_bf, o_a, o_b, o_c, wts)
        u3 = u_p.reshape(batch, seq, IN_WIDTH)
        st_p[0].append(u3[:, :, OFF_KA:OFF_KA + DA_K].reshape(batch, seq, DA_KV_HEADS, 2 * DA_HEAD_DIM))
        st_p[1].append(u3[:, :, OFF_VA:OFF_VA + DA_V].reshape(batch, seq, DA_KV_HEADS, 2 * DA_HEAD_DIM))
        st_p[2].append(u3[:, :, OFF_KB:OFF_KB + MB_W].reshape(batch, seq, MB_HEADS, MB_HEAD_DIM))
        st_p[3].append(u3[:, :, OFF_VB:OFF_VB + MB_W].reshape(batch, seq, MB_HEADS, MB_HEAD_DIM))
        st_p[4].append(tail[:, SUBLANES - (CONV_W - 1):, :])

        u_s = matmul(xs_bf, w_in, l)[:n_s].reshape(db, t_len, IN_WIDTH)
        seg = lambda off, width: u_s[:, :, off:off + width]
        q_a = seg(OFF_QA, DA_Q).reshape(db, t_len, DA_KV_HEADS, 2, 2, DA_HEAD_DIM)
        q_a = q_a.transpose(0, 2, 4, 3, 1, 5).reshape(db, 4 * DA_KV_HEADS * t_len, DA_HEAD_DIM)
        k_a, v_a = seg(OFF_KA, DA_K), seg(OFF_VA, DA_V)
        new_rows = lambda a: split_component_view(a.reshape(db, t_len, DA_KV_HEADS, 2 * DA_HEAD_DIM)).reshape(
            db, t_len * 2 * DA_KV_HEADS, DA_HEAD_DIM)
        oa_s = diff_attn_sample(q_a, ck_a, cv_a, new_rows(k_a), new_rows(v_a), page_table, lam_p, sub_g,
                                l, lam_init, t_len)
        oa_s = oa_s.reshape(db, DA_KV_HEADS, 2, t_len, 2 * DA_HEAD_DIM).transpose(0, 3, 1, 2, 4)
        oa_s = oa_s.reshape(n_s, DA_Q)

        q_b = seg(OFF_QB, MB_W).reshape(db, t_len, MB_HEADS, MB_HEAD_DIM)
        k_b, v_b = seg(OFF_KB, MB_W), seg(OFF_VB, MB_W)
        head_major = lambda a: a.transpose(0, 2, 1, 3).reshape(db, MB_HEADS * t_len, a.shape[-1])
        token_head = lambda a: a.reshape(db, t_len * MB_HEADS, MB_HEAD_DIM)
        gs = moba_gate_sample(q_b, ck_b, page_table, l)
        ob_s = moba_attn_sample(head_major(q_b), head_major(gs), ck_b, cv_b, token_head(k_b), token_head(v_b),
                                page_table, l)
        ob_s = ob_s.reshape(db, MB_HEADS, t_len, MB_HEAD_DIM).transpose(0, 2, 1, 3).reshape(n_s, MB_W)

        tmaj = lambda off: seg(off, CONV_DIM).transpose(1, 0, 2)
        oc_s, nst = conv_sample(tmaj(OFF_XC), tmaj(OFF_GB), tmaj(OFF_GC), state_conv[l].transpose(1, 0, 2),
                                conv_w[l])
        oc_s = oc_s.transpose(1, 0, 2).reshape(n_s, CONV_DIM)

        pad_tok = lambda a: jnp.pad(a, ((0, pad_s - n_s), (0, 0))).astype(BF16)
        xs, xs_bf = _token_stage(xs, xs_bf, pad_tok(oa_s), pad_tok(ob_s), pad_tok(oc_s), wts)
        st_s[0].append(k_a.reshape(db, t_len, DA_KV_HEADS, 2 * DA_HEAD_DIM))
        st_s[1].append(v_a.reshape(db, t_len, DA_KV_HEADS, 2 * DA_HEAD_DIM))
        st_s[2].append(k_b.reshape(db, t_len, MB_HEADS, MB_HEAD_DIM))
        st_s[3].append(v_b.reshape(db, t_len, MB_HEADS, MB_HEAD_DIM))
        st_s[4].append(nst.transpose(1, 0, 2))

    y_p = xp.reshape(batch, seq, d)
    y_s = xs[:n_s].reshape(db, t_len, d)
    stack = lambda parts: jnp.stack(parts, axis=0)
    return (y_p, y_s, stack(st_p[0]), stack(st_p[1]), stack(st_p[2]), stack(st_p[3]), stack(st_p[4]),
            stack(st_s[0]), stack(st_s[1]), stack(st_s[2]), stack(st_s[3]), stack(st_s[4]))
```
